```python
import jax, jax.numpy as jnp
from jax import lax
import numpy as np

D_MODEL = 4096
BATCH = 1
SEQ = 8192
DEPTH = 1
DEC_BATCH = 32
DEC_SEQ = 8
PAST_LEN = 8192
PAGE_SIZE = 128

HEAD_DIM = 128
N_HEADS = D_MODEL // 256
D_ATTN = N_HEADS * HEAD_DIM
D_CONV = D_MODEL // 2
CONV_WIDTH = 31
FFN_CONV_WIDTH = 3
D_FF = ((8 * D_MODEL) // 3 + 255) // 256 * 256
Q_BLOCK = 128
SB_BIAS_INIT = -9.0
RMS_EPS = 1e-6
LN_EPS = 1e-5

kernel_name = "gated_conformer_stickbreaking_convffn_step"


def rmsnorm(x, g):
    xf = x.astype(jnp.float32)
    y = xf * lax.rsqrt(jnp.mean(xf * xf, axis=-1, keepdims=True) + RMS_EPS)
    return (y * g.astype(jnp.float32)).astype(x.dtype)


def layernorm(x, g, b):
    xf = x.astype(jnp.float32)
    mu = jnp.mean(xf, axis=-1, keepdims=True)
    xc = xf - mu
    y = xc * lax.rsqrt(jnp.mean(xc * xc, axis=-1, keepdims=True) + LN_EPS)
    return (y * g.astype(jnp.float32) + b.astype(jnp.float32)).astype(x.dtype)


def causal_dwconv(hist, x_new, w, b):
    xp = jnp.concatenate([hist, x_new], axis=1)
    c = x_new.shape[-1]
    y = lax.conv_general_dilated(xp, w[:, None, :], window_strides=(1,), padding='VALID',
                                 dimension_numbers=('NWC', 'WIO', 'NWC'),
                                 feature_group_count=c)
    return y + b, xp[:, -(w.shape[0] - 1):]


def stick_breaking(q, k, v, sb_bias, q_pos, k_pos):
    z = jnp.einsum('bqhd,bkhd->bhqk', q, k, preferred_element_type=jnp.float32) * (HEAD_DIM ** -0.5)
    z = z + sb_bias.astype(jnp.float32)[None, :, None, None]
    causal = k_pos[None, :] < q_pos[:, None]
    log_beta = jax.nn.log_sigmoid(z)
    log_1mb = jnp.where(causal, jax.nn.log_sigmoid(-z), 0.0)
    tail = lax.cumsum(log_1mb, axis=3, reverse=True) - log_1mb
    a = jnp.where(causal, jnp.exp(log_beta + tail), 0.0)
    o = jnp.einsum('bhqk,bkhd->bqhd', a, v.astype(jnp.float32))
    return o.astype(v.dtype)


def stick_breaking_blocked(q, k, v, sb_bias):
    b, t, h, dh = q.shape
    nb = t // Q_BLOCK
    qb = q.reshape(b, nb, Q_BLOCK, h, dh).transpose(1, 0, 2, 3, 4)
    k_pos = jnp.arange(t)

    def one_block(args):
        qi, i = args
        q_pos = i * Q_BLOCK + jnp.arange(Q_BLOCK)
        return stick_breaking(qi, k, v, sb_bias, q_pos, k_pos)

    o = lax.map(one_block, (qb, jnp.arange(nb)))
    return o.transpose(1, 0, 2, 3, 4).reshape(b, t, h, dh)


def decoder_layer(x, conv_hist, ffn_hist, k_past, v_past,
                  norm1_g, w_in, dw_a_w, dw_a_b, ln_a_g, ln_a_b, w_proj_a,
                  q_norm_g, k_norm_g, sb_bias, w_proj_b, w_out,
                  norm2_g, w_up, dw_f_w, dw_f_b, w_down):
    b, t, _ = x.shape
    h = rmsnorm(x, norm1_g)
    u = h @ w_in
    glu_in, q, k, v, g = jnp.split(
        u, [2 * D_CONV, 2 * D_CONV + D_ATTN, 2 * D_CONV + 2 * D_ATTN, 2 * D_CONV + 3 * D_ATTN], axis=-1)
    gate_a = jax.nn.sigmoid(g[..., :D_MODEL])
    gate_b = jax.nn.sigmoid(g[..., D_MODEL:])

    a_glu = glu_in[..., :D_CONV] * jax.nn.sigmoid(glu_in[..., D_CONV:])
    a_conv, conv_state = causal_dwconv(conv_hist, a_glu, dw_a_w, dw_a_b)
    a_out = jax.nn.silu(layernorm(a_conv, ln_a_g, ln_a_b)) @ w_proj_a

    q = rmsnorm(q.reshape(b, t, N_HEADS, HEAD_DIM), q_norm_g)
    k = rmsnorm(k.reshape(b, t, N_HEADS, HEAD_DIM), k_norm_g)
    v = v.reshape(b, t, N_HEADS, HEAD_DIM)
    if k_past is None:
        o = stick_breaking_blocked(q, k, v, sb_bias)
    else:
        p = k_past.shape[1]
        k_all = jnp.concatenate([k_past, k], axis=1)
        v_all = jnp.concatenate([v_past, v], axis=1)
        o = stick_breaking(q, k_all, v_all, sb_bias, p + jnp.arange(t), jnp.arange(p + t))
    b_out = o.reshape(b, t, D_ATTN) @ w_proj_b

    x = x + (gate_a * a_out + gate_b * b_out) @ w_out

    h2 = rmsnorm(x, norm2_g)
    f_a, f_b = jnp.split(h2 @ w_up, 2, axis=-1)
    f_conv, ffn_state = causal_dwconv(ffn_hist, f_a, dw_f_w, dw_f_b)
    y = x + (jax.nn.silu(f_conv) * f_b) @ w_down
    return y, k, v, conv_state, ffn_state


def setup_inputs(seed: int = 0) -> dict:
    key = jax.random.key(seed)
    ks = jax.random.split(key, 32)
    f32 = jnp.float32
    n_pages = PAST_LEN // PAGE_SIZE
    n_phys = (5 * DEC_BATCH * n_pages) // 4
    perm = jax.random.permutation(ks[0], n_phys)[:DEC_BATCH * n_pages]
    page_table = perm.reshape(DEC_BATCH, n_pages).astype(jnp.int32)
    d_in = 2 * D_CONV + 3 * D_ATTN + 2 * D_MODEL

    def w(k, shape, fan_in):
        return jax.random.normal(k, shape, f32) * (fan_in ** -0.5)

    def gain(k, n):
        return 1.0 + 0.02 * jax.random.normal(k, (n,), f32)

    def bias(k, n):
        return 0.02 * jax.random.normal(k, (n,), f32)

    return {
        "x_prompt": jax.random.normal(ks[1], (BATCH, SEQ, D_MODEL), f32),
        "x_sample": jax.random.normal(ks[2], (DEC_BATCH, DEC_SEQ, D_MODEL), f32),
        "cache_k": jax.random.normal(ks[3], (n_phys, PAGE_SIZE, N_HEADS, HEAD_DIM), f32),
        "cache_v": jax.random.normal(ks[4], (n_phys, PAGE_SIZE, N_HEADS, HEAD_DIM), f32),
        "state_conv": jax.random.normal(ks[5], (DEC_BATCH, CONV_WIDTH - 1, D_CONV), f32),
        "state_ffn_conv": jax.random.normal(ks[6], (DEC_BATCH, FFN_CONV_WIDTH - 1, D_FF), f32),
        "page_table": page_table,
        "norm1_g": gain(ks[7], D_MODEL),
        "w_in": w(ks[8], (D_MODEL, d_in), D_MODEL),
        "dw_a_w": w(ks[9], (CONV_WIDTH, D_CONV), CONV_WIDTH),
        "dw_a_b": bias(ks[10], D_CONV),
        "ln_a_g": gain(ks[11], D_CONV),
        "ln_a_b": bias(ks[12], D_CONV),
        "w_proj_a": w(ks[13], (D_CONV, D_MODEL), D_CONV),
        "q_norm_g": gain(ks[14], HEAD_DIM),
        "k_norm_g": gain(ks[15], HEAD_DIM),
        "sb_bias": SB_BIAS_INIT + 0.1 * jax.random.normal(ks[23], (N_HEADS,), f32),
        "w_proj_b": w(ks[16], (D_ATTN, D_MODEL), D_ATTN),
        "w_out": w(ks[17], (D_MODEL, D_MODEL), D_MODEL),
        "norm2_g": gain(ks[18], D_MODEL),
        "w_up": w(ks[19], (D_MODEL, 2 * D_FF), D_MODEL),
        "dw_f_w": w(ks[20], (FFN_CONV_WIDTH, D_FF), FFN_CONV_WIDTH),
        "dw_f_b": bias(ks[21], D_FF),
        "w_down": w(ks[22], (D_FF, D_MODEL), D_FF),
    }


def reference(x_prompt, x_sample, cache_k, cache_v, state_conv, state_ffn_conv, page_table,
              norm1_g, w_in, dw_a_w, dw_a_b, ln_a_g, ln_a_b, w_proj_a,
              q_norm_g, k_norm_g, sb_bias, w_proj_b, w_out,
              norm2_g, w_up, dw_f_w, dw_f_b, w_down):
    weights = (norm1_g, w_in, dw_a_w, dw_a_b, ln_a_g, ln_a_b, w_proj_a,
               q_norm_g, k_norm_g, sb_bias, w_proj_b, w_out,
               norm2_g, w_up, dw_f_w, dw_f_b, w_down)
    yp, ysm = x_prompt, x_sample
    for _ in range(DEPTH):
        conv0 = jnp.zeros((yp.shape[0], CONV_WIDTH - 1, D_CONV), yp.dtype)
        ffn0 = jnp.zeros((yp.shape[0], FFN_CONV_WIDTH - 1, D_FF), yp.dtype)
        yp, k_p, v_p, conv_p, ffn_p = decoder_layer(yp, conv0, ffn0, None, None, *weights)

        db, n_pages = page_table.shape
        k_past = cache_k[page_table].reshape(db, n_pages * PAGE_SIZE, N_HEADS, HEAD_DIM)
        v_past = cache_v[page_table].reshape(db, n_pages * PAGE_SIZE, N_HEADS, HEAD_DIM)
        ysm, k_s, v_s, conv_s, ffn_s = decoder_layer(ysm, state_conv, state_ffn_conv,
                                                     k_past, v_past, *weights)
    return (yp, ysm, k_p, v_p, conv_p, ffn_p, k_s, v_s, conv_s, ffn_s)
```

```python
import functools
import math

import jax
import jax.numpy as jnp
from jax import lax
from jax.experimental import pallas as pl
from jax.experimental.pallas import tpu as pltpu

HEAD_DIM = 128
RMS_EPS = 1e-6
LN_EPS = 1e-5
CONV_HALO = 32
SUBLANES = 8
V7X_VMEM_LIMIT = 56 * 1024 * 1024

F32 = jnp.float32
BF16 = jnp.bfloat16


def _largest_divisor(n, candidates):
    for c in candidates:
        if n % c == 0:
            return c
    raise ValueError(f"no tile in {candidates} divides {n}")


def _dot(a, b):
    return jnp.dot(a, b, preferred_element_type=F32)


def _sigmoid(x):
    return 1.0 / (1.0 + jnp.exp(-x))


def _params(*sem):
    return pltpu.CompilerParams(dimension_semantics=sem, vmem_limit_bytes=V7X_VMEM_LIMIT)


def _rmsnorm_kernel(x_ref, g_ref, o_ref):
    x = x_ref[...]
    ms = jnp.mean(x * x, axis=-1, keepdims=True)
    o_ref[...] = (x * lax.rsqrt(ms + RMS_EPS) * g_ref[...]).astype(o_ref.dtype)


def _rmsnorm(x, g):
    m, d = x.shape
    tr = _largest_divisor(m, (256, 128, 64, 32, 16, 8))
    return pl.pallas_call(
        _rmsnorm_kernel,
        grid=(m // tr,),
        in_specs=[pl.BlockSpec((tr, d), lambda i: (i, 0)),
                  pl.BlockSpec((1, d), lambda i: (0, 0))],
        out_specs=pl.BlockSpec((tr, d), lambda i: (i, 0)),
        out_shape=jax.ShapeDtypeStruct((m, d), BF16),
        compiler_params=_params("parallel"),
    )(x, g.reshape(1, d))


def _glu_kernel(h_ref, wa_ref, wb_ref, o_ref):
    h = h_ref[...]
    o_ref[...] = _dot(h, wa_ref[...]) * _sigmoid(_dot(h, wb_ref[...]))


def _headnorm_kernel(h_ref, w_ref, g_ref, *o_refs, scale, emit_f32):
    acc = _dot(h_ref[...], w_ref[...])
    g = g_ref[...]
    for hh in range(acc.shape[1] // HEAD_DIM):
        blk = acc[:, hh * HEAD_DIM:(hh + 1) * HEAD_DIM]
        ms = jnp.mean(blk * blk, axis=-1, keepdims=True)
        y = blk * lax.rsqrt(ms + RMS_EPS) * g
        cols = slice(hh * HEAD_DIM, (hh + 1) * HEAD_DIM)
        if emit_f32:
            o_refs[0][:, cols] = y
        o_refs[-1][:, cols] = (y * scale).astype(BF16)


def _plain_kernel(h_ref, w_ref, o32_ref, o16_ref):
    acc = _dot(h_ref[...], w_ref[...])
    o32_ref[...] = acc
    o16_ref[...] = acc.astype(BF16)


def _gate_kernel(h_ref, w_ref, o_ref):
    o_ref[...] = _sigmoid(_dot(h_ref[...], w_ref[...])).astype(BF16)


def _in_proj(kernel, h, w, col0, n_cols, out_dtypes, extra=(), n_w=1, w_stride=0):
    m, k = h.shape
    tm = _largest_divisor(m, (1024, 512, 256))
    tn = _largest_divisor(math.gcd(n_cols, col0, w_stride), (512, 256, 128))
    w_specs = [pl.BlockSpec((k, tn), functools.partial(
        lambda i, j, off: (0, j + off), off=(col0 + s * w_stride) // tn)) for s in range(n_w)]
    extra_specs = [pl.BlockSpec(e.shape, lambda i, j: (0, 0)) for e in extra]
    outs = pl.pallas_call(
        kernel,
        grid=(m // tm, n_cols // tn),
        in_specs=[pl.BlockSpec((tm, k), lambda i, j: (i, 0))] + w_specs + extra_specs,
        out_specs=[pl.BlockSpec((tm, tn), lambda i, j: (i, j)) for _ in out_dtypes],
        out_shape=[jax.ShapeDtypeStruct((m, n_cols), dt) for dt in out_dtypes],
        compiler_params=_params("parallel", "parallel"),
    )(h, *([w] * n_w), *extra)
    return outs


def _conv_ln_silu(xp_ref, conv_ref, w_ref, b_ref, g_ref, beta_ref, o_ref, *, rows, base):
    width, c = w_ref.shape
    lane_chunk = 128
    row_chunk = min(rows, 32)
    n_shift = SUBLANES

    def chunk(ci, carry):
        lanes = pl.ds(pl.multiple_of(ci * lane_chunk, lane_chunk), lane_chunk)
        wv = w_ref[:, lanes]
        bias = b_ref[:, lanes]
        for r0 in range(0, rows, row_chunk):
            acc = jnp.zeros((row_chunk, lane_chunk), F32) + bias
            for b in range(min(n_shift, width)):
                taps = range(b, width, n_shift)
                span = row_chunk + (len(taps) - 1) * n_shift
                shifted = xp_ref[pl.ds(base + r0 + b, span), lanes]
                for tap in taps:
                    acc = acc + shifted[tap - b:tap - b + row_chunk] * wv[tap:tap + 1]
            conv_ref[pl.ds(r0, row_chunk), lanes] = acc
        return carry

    lax.fori_loop(0, c // lane_chunk, chunk, 0)

    g = g_ref[...]
    beta = beta_ref[...]
    for r0 in range(0, rows, row_chunk):
        x = conv_ref[pl.ds(r0, row_chunk), :]
        mu = jnp.mean(x, axis=-1, keepdims=True)
        xc = x - mu
        var = jnp.mean(xc * xc, axis=-1, keepdims=True)
        y = xc * lax.rsqrt(var + LN_EPS) * g + beta
        o_ref[pl.ds(r0, row_chunk), :] = (y * _sigmoid(y)).astype(o_ref.dtype)


def _conv_prompt_kernel(halo_ref, x_ref, w_ref, b_ref, g_ref, beta_ref, o_ref, xp_ref, conv_ref, *, width):
    rows = x_ref.shape[0]
    first = pl.program_id(0) == 0
    xp_ref[pl.ds(0, CONV_HALO), :] = jnp.where(first, 0.0, halo_ref[...])
    xp_ref[pl.ds(CONV_HALO, rows), :] = x_ref[...]
    _conv_ln_silu(xp_ref, conv_ref, w_ref, b_ref, g_ref, beta_ref, o_ref,
                  rows=rows, base=CONV_HALO - (width - 1))


def _conv_sample_kernel(hist_ref, x_ref, w_ref, b_ref, g_ref, beta_ref, o_ref, xp_ref, conv_ref, *, width):
    rows = x_ref.shape[1]
    xp_ref[...] = jnp.zeros_like(xp_ref)
    xp_ref[pl.ds(0, width - 1), :] = hist_ref[0]
    xp_ref[pl.ds(width - 1, rows), :] = x_ref[0]
    _conv_ln_silu(xp_ref, conv_ref, w_ref, b_ref, g_ref, beta_ref, o_ref.at[0],
                  rows=rows, base=0)


def _conv_branch_prompt(a_glu, w, b, g, beta):
    t, c = a_glu.shape
    width = w.shape[0]
    assert width - 1 <= CONV_HALO
    tr = _largest_divisor(t, (256, 128, 64, 32))
    halo_blocks = tr // CONV_HALO
    vec = lambda v: v.reshape(1, c)
    return pl.pallas_call(
        functools.partial(_conv_prompt_kernel, width=width),
        grid=(t // tr,),
        in_specs=[pl.BlockSpec((CONV_HALO, c), lambda i: (jnp.maximum(i * halo_blocks - 1, 0), 0)),
                  pl.BlockSpec((tr, c), lambda i: (i, 0)),
                  pl.BlockSpec((width, c), lambda i: (0, 0))] +
                 [pl.BlockSpec((1, c), lambda i: (0, 0))] * 3,
        out_specs=pl.BlockSpec((tr, c), lambda i: (i, 0)),
        out_shape=jax.ShapeDtypeStruct((t, c), BF16),
        scratch_shapes=[pltpu.VMEM((CONV_HALO + tr, c), F32), pltpu.VMEM((tr, c), F32)],
        compiler_params=_params("parallel"),
    )(a_glu, a_glu, w, vec(b), vec(g), vec(beta))


def _conv_branch_sample(hist, a_glu, w, b, g, beta):
    nb, t, c = a_glu.shape
    width = w.shape[0]
    xp_rows = -(-(width - 1 + t + SUBLANES) // SUBLANES) * SUBLANES
    vec = lambda v: v.reshape(1, c)
    return pl.pallas_call(
        functools.partial(_conv_sample_kernel, width=width),
        grid=(nb,),
        in_specs=[pl.BlockSpec((1, width - 1, c), lambda i: (i, 0, 0)),
                  pl.BlockSpec((1, t, c), lambda i: (i, 0, 0)),
                  pl.BlockSpec((width, c), lambda i: (0, 0))] +
                 [pl.BlockSpec((1, c), lambda i: (0, 0))] * 3,
        out_specs=pl.BlockSpec((1, t, c), lambda i: (i, 0, 0)),
        out_shape=jax.ShapeDtypeStruct((nb, t, c), BF16),
        scratch_shapes=[pltpu.VMEM((xp_rows, c), F32), pltpu.VMEM((t, c), F32)],
        compiler_params=_params("parallel"),
    )(hist, a_glu, w, vec(b), vec(g), vec(beta))


def _log_gates(z):
    lp = jnp.log(1.0 + jnp.exp(-jnp.abs(z)))
    log_beta = jnp.minimum(z, 0.0) - lp
    return log_beta, log_beta - z


def _attn_prompt_kernel(bias_ref, q_ref, k_ref, v_ref, u_ref, o_ref, *, tk):
    hd = pl.program_id(0)
    qi = pl.program_id(1)
    tq = q_ref.shape[0]
    assert tq == tk
    bias = bias_ref[hd]
    q = q_ref[...]
    u = u_ref[...]

    def block(kb):
        rows = pl.ds(pl.multiple_of(kb * tk, tk), tk)
        z = lax.dot_general(q, k_ref[rows, :], (((1,), (1,)), ((), ())),
                            preferred_element_type=F32) + bias
        return _log_gates(z), v_ref[rows, :]

    (log_beta, log_1mb), v = block(qi)
    causal = lax.broadcasted_iota(jnp.int32, (tq, tk), 1) < lax.broadcasted_iota(jnp.int32, (tq, tk), 0)
    log_1mb = jnp.where(causal, log_1mb, 0.0)
    tail = _dot(log_1mb.astype(BF16), u)
    a = jnp.where(causal, jnp.exp(log_beta + tail), 0.0)
    acc = _dot(a.astype(BF16), v)
    run = jnp.sum(log_1mb, axis=1, keepdims=True)

    def step(t, state):
        acc, run = state
        (log_beta, log_1mb), v = block(qi - 1 - t)
        tail = _dot(log_1mb.astype(BF16), u)
        a = jnp.exp(log_beta + tail + run)
        return acc + _dot(a.astype(BF16), v), run + jnp.sum(log_1mb, axis=1, keepdims=True)

    acc, _ = lax.fori_loop(0, qi, step, (acc, run))
    o_ref[...] = acc.astype(o_ref.dtype)


def _attn_prompt(q, k, v, sb_bias):
    t, d = q.shape
    nh = d // HEAD_DIM
    tq = _largest_divisor(t, (256, 128))
    tri = (jnp.arange(tq)[:, None] > jnp.arange(tq)[None, :]).astype(BF16)
    return pl.pallas_call(
        functools.partial(_attn_prompt_kernel, tk=tq),
        grid=(nh, t // tq),
        in_specs=[pl.BlockSpec(memory_space=pltpu.SMEM),
                  pl.BlockSpec((tq, HEAD_DIM), lambda h, i: (i, h)),
                  pl.BlockSpec((t, HEAD_DIM), lambda h, i: (0, h)),
                  pl.BlockSpec((t, HEAD_DIM), lambda h, i: (0, h)),
                  pl.BlockSpec((tq, tq), lambda h, i: (0, 0))],
        out_specs=pl.BlockSpec((tq, HEAD_DIM), lambda h, i: (i, h)),
        out_shape=jax.ShapeDtypeStruct((t, d), BF16),
        compiler_params=_params("parallel", "parallel"),
    )(sb_bias, q, k, v, tri)


def _attn_sample_kernel(pt_ref, qt_ref, bias_ref, knew_ref, vnew_ref, ut_ref, *rest, n_slots, t_new):
    k_refs = rest[:n_slots]
    v_refs = rest[n_slots:2 * n_slots]
    o_ref = rest[2 * n_slots]
    qbd_ref, run_ref, acc_ref, pad_k_ref, pad_v_ref = rest[2 * n_slots + 1:]
    p = pl.program_id(1)
    page, d = pad_k_ref.shape
    ncol = qbd_ref.shape[1]
    nh = d // HEAD_DIM
    bias = bias_ref[...]
    ut = ut_ref[...]

    def visit(k, v, mask):
        z = _dot(k, qbd_ref[...]) + bias
        log_beta, log_1mb = _log_gates(z)
        if mask is not None:
            log_1mb = jnp.where(mask, log_1mb, 0.0)
        tail = _dot(ut, log_1mb.astype(BF16))
        a = jnp.exp(log_beta + tail + run_ref[...])
        if mask is not None:
            a = jnp.where(mask, a, 0.0)
        acc_ref[...] += lax.dot_general(a.astype(BF16), v, (((0,), (0,)), ((), ())),
                                        preferred_element_type=F32)
        run_ref[...] += jnp.sum(log_1mb, axis=0, keepdims=True)

    @pl.when(p == 0)
    def _():
        row_head = lax.broadcasted_iota(jnp.int32, (d, ncol), 0) // HEAD_DIM
        col_head = lax.broadcasted_iota(jnp.int32, (d, ncol), 1) // t_new
        qbd_ref[...] = jnp.where(row_head == col_head, qt_ref[0], jnp.zeros((), BF16))
        run_ref[...] = jnp.zeros_like(run_ref)
        acc_ref[...] = jnp.zeros_like(acc_ref)
        pad_k_ref[...] = jnp.zeros_like(pad_k_ref)
        pad_v_ref[...] = jnp.zeros_like(pad_v_ref)
        pad_k_ref[pl.ds(0, t_new), :] = knew_ref[0]
        pad_v_ref[pl.ds(0, t_new), :] = vnew_ref[0]
        key_t = lax.broadcasted_iota(jnp.int32, (page, ncol), 0)
        qry_t = lax.broadcasted_iota(jnp.int32, (page, ncol), 1) % t_new
        visit(pad_k_ref[...].astype(BF16), pad_v_ref[...].astype(BF16), key_t < qry_t)

    for s in range(n_slots):
        visit(k_refs[s][0].astype(BF16), v_refs[s][0].astype(BF16), None)

    @pl.when(p == pl.num_programs(1) - 1)
    def _():
        for hh in range(nh):
            o_ref[0, :, hh * HEAD_DIM:(hh + 1) * HEAD_DIM] = acc_ref[
                hh * t_new:(hh + 1) * t_new, hh * HEAD_DIM:(hh + 1) * HEAD_DIM].astype(o_ref.dtype)


def _attn_sample(q, k_new, v_new, cache_k, cache_v, page_table, sb_bias):
    nb, t_new, d = q.shape
    nh = d // HEAD_DIM
    n_pages = page_table.shape[1]
    page = cache_k.shape[1]
    ncol = nh * t_new
    n_slots = _largest_divisor(n_pages, (4, 2, 1))
    q_t = jnp.tile(jnp.swapaxes(q, 1, 2), (1, 1, nh))
    bias_cols = jnp.repeat(sb_bias, t_new).reshape(1, ncol)
    tri_t = (jnp.arange(page)[None, :] > jnp.arange(page)[:, None]).astype(BF16)

    def page_spec(s):
        return pl.BlockSpec((1, page, d),
                            lambda b, p, pt: (pt[b, n_pages - 1 - p * n_slots - s], 0, 0))

    grid_spec = pltpu.PrefetchScalarGridSpec(
        num_scalar_prefetch=1,
        grid=(nb, n_pages // n_slots),
        in_specs=[pl.BlockSpec((1, d, ncol), lambda b, p, pt: (b, 0, 0)),
                  pl.BlockSpec((1, ncol), lambda b, p, pt: (0, 0)),
                  pl.BlockSpec((1, t_new, d), lambda b, p, pt: (b, 0, 0)),
                  pl.BlockSpec((1, t_new, d), lambda b, p, pt: (b, 0, 0)),
                  pl.BlockSpec((page, page), lambda b, p, pt: (0, 0))] +
                 [page_spec(s) for s in range(n_slots)] * 2,
        out_specs=pl.BlockSpec((1, t_new, d), lambda b, p, pt: (b, 0, 0)),
        scratch_shapes=[pltpu.VMEM((d, ncol), BF16), pltpu.VMEM((1, ncol), F32),
                        pltpu.VMEM((ncol, d), F32), pltpu.VMEM((page, d), F32),
                        pltpu.VMEM((page, d), F32)],
    )
    return pl.pallas_call(
        functools.partial(_attn_sample_kernel, n_slots=n_slots, t_new=t_new),
        grid_spec=grid_spec,
        out_shape=jax.ShapeDtypeStruct((nb, t_new, d), BF16),
        compiler_params=_params("parallel", "arbitrary"),
    )(page_table, q_t, bias_cols, k_new, v_new, tri_t,
      *([cache_k] * n_slots), *([cache_v] * n_slots))


def _merge_kernel(a_ref, o_ref, wa_ref, wb_ref, ga_ref, gb_ref, out_ref):
    ya = _dot(a_ref[...], wa_ref[...])
    yb = _dot(o_ref[...], wb_ref[...])
    out_ref[...] = (ga_ref[...].astype(F32) * ya + gb_ref[...].astype(F32) * yb).astype(out_ref.dtype)


def _merge(act_a, o, w_a, w_b, gates):
    m, ka = act_a.shape
    kb = o.shape[1]
    n = w_a.shape[1]
    tm = _largest_divisor(m, (1024, 512, 256))
    tn = _largest_divisor(n, (512, 256))
    return pl.pallas_call(
        _merge_kernel,
        grid=(m // tm, n // tn),
        in_specs=[pl.BlockSpec((tm, ka), lambda i, j: (i, 0)),
                  pl.BlockSpec((tm, kb), lambda i, j: (i, 0)),
                  pl.BlockSpec((ka, tn), lambda i, j: (0, j)),
                  pl.BlockSpec((kb, tn), lambda i, j: (0, j)),
                  pl.BlockSpec((tm, tn), lambda i, j: (i, j)),
                  pl.BlockSpec((tm, tn), lambda i, j: (i, j + n // tn))],
        out_specs=pl.BlockSpec((tm, tn), lambda i, j: (i, j)),
        out_shape=jax.ShapeDtypeStruct((m, n), BF16),
        compiler_params=_params("parallel", "parallel"),
    )(act_a, o, w_a, w_b, gates, gates)


def _residual_kernel(a_ref, w_ref, x_ref, o_ref):
    o_ref[...] = x_ref[...] + _dot(a_ref[...], w_ref[...])


def _residual_matmul(a, w, x):
    m, k = a.shape
    n = w.shape[1]
    tm = _largest_divisor(m, (512, 256))
    tn = _largest_divisor(n, (512, 256))
    return pl.pallas_call(
        _residual_kernel,
        grid=(m // tm, n // tn),
        in_specs=[pl.BlockSpec((tm, k), lambda i, j: (i, 0)),
                  pl.BlockSpec((k, tn), lambda i, j: (0, j)),
                  pl.BlockSpec((tm, tn), lambda i, j: (i, j))],
        out_specs=pl.BlockSpec((tm, tn), lambda i, j: (i, j)),
        out_shape=jax.ShapeDtypeStruct((m, n), F32),
        compiler_params=_params("parallel", "parallel"),
    )(a, w, x)


def _ffn_up_kernel(h_ref, wa_ref, wb_ref, cw_ref, cb_ref, *rest, group, tail_rows):
    if group is None:
        g_ref, tail_ref, carry_ref = rest
    else:
        hist_ref, g_ref, tail_ref = rest
    i = pl.program_id(0)
    j = pl.program_id(1)
    h = h_ref[...]
    f_a = _dot(h, wa_ref[...])
    f_b = _dot(h, wb_ref[...])
    tm = f_a.shape[0]
    width = cw_ref.shape[0]
    row = lax.broadcasted_iota(jnp.int32, f_a.shape, 0)
    conv = f_a * cw_ref[width - 1:width, :] + cb_ref[...]
    if group is None:
        @pl.when(i == 0)
        def _():
            carry_ref[j] = jnp.zeros(carry_ref.shape[1:], F32)

        prev_tail = carry_ref[j]
        for d in range(1, width):
            shifted = pltpu.roll(f_a, d, axis=0)
            for r in range(d):
                shifted = jnp.where(row == r, prev_tail[SUBLANES - d + r:SUBLANES - d + r + 1, :], shifted)
            conv = conv + shifted * cw_ref[width - 1 - d:width - d, :]
        carry_ref[j] = f_a[tm - SUBLANES:, :]
    else:
        hist = hist_ref[...]
        pos = row % group
        for d in range(1, width):
            shifted = jnp.where(pos < d, pltpu.roll(hist, (tm - (width - 1) + d) % tm, axis=0),
                                pltpu.roll(f_a, d, axis=0))
            conv = conv + shifted * cw_ref[width - 1 - d:width - d, :]
    g_ref[...] = (conv * _sigmoid(conv) * f_b).astype(g_ref.dtype)
    tail_ref[0] = f_a[tm - tail_rows:, :]


def _ffn_up(h2, w_up, conv_w, conv_b, hist_rows=None, group=None):
    m, k = h2.shape
    f = w_up.shape[1] // 2
    width = conv_w.shape[0]
    tm = _largest_divisor(m, (1024, 512, 256))
    tn = _largest_divisor(f, (256, 128))
    nj = f // tn
    tail_rows = SUBLANES if group is None else tm
    assert width - 1 <= SUBLANES
    in_specs = [pl.BlockSpec((tm, k), lambda i, j: (i, 0)),
                pl.BlockSpec((k, tn), lambda i, j: (0, j)),
                pl.BlockSpec((k, tn), lambda i, j: (0, j + nj)),
                pl.BlockSpec((width, tn), lambda i, j: (0, j)),
                pl.BlockSpec((1, tn), lambda i, j: (0, j))]
    args = [h2, w_up, w_up, conv_w, conv_b.reshape(1, f)]
    scratch = []
    if group is None:
        scratch = [pltpu.VMEM((nj, SUBLANES, tn), F32)]
    else:
        assert m == tm
        in_specs.append(pl.BlockSpec((tm, tn), lambda i, j: (i, j)))
        args.append(hist_rows)
    return pl.pallas_call(
        functools.partial(_ffn_up_kernel, group=group, tail_rows=tail_rows),
        grid=(m // tm, nj),
        in_specs=in_specs,
        out_specs=[pl.BlockSpec((tm, tn), lambda i, j: (i, j)),
                   pl.BlockSpec((1, tail_rows, tn), lambda i, j: (i, 0, j))],
        out_shape=[jax.ShapeDtypeStruct((m, f), BF16),
                   jax.ShapeDtypeStruct((m // tm, tail_rows, f), F32)],
        scratch_shapes=scratch,
        compiler_params=_params("arbitrary", "arbitrary"),
    )(*args)


def _layer(x, wts, *, conv_hist=None, ffn_hist=None, paged=None):
    (norm1_g, w_in, dw_a_w, dw_a_b, ln_a_g, ln_a_b, w_proj_a, q_norm_g, k_norm_g, sb_bias,
     w_proj_b, w_out, norm2_g, w_up, dw_f_w, dw_f_b, w_down) = wts
    nb, t, d_model = x.shape
    m = nb * t
    d_conv = dw_a_w.shape[1]
    d_attn = w_proj_b.shape[0]
    d_ff = dw_f_w.shape[1]
    x2d = x.reshape(m, d_model)

    h = _rmsnorm(x2d, norm1_g)
    (a_glu,) = _in_proj(_glu_kernel, h, w_in, 0, d_conv, [F32], n_w=2, w_stride=d_conv)
    col = 2 * d_conv
    (q16,) = _in_proj(functools.partial(_headnorm_kernel, scale=HEAD_DIM ** -0.5, emit_f32=False),
                      h, w_in, col, d_attn, [BF16], extra=(q_norm_g.reshape(1, HEAD_DIM),))
    k32, k16 = _in_proj(functools.partial(_headnorm_kernel, scale=1.0, emit_f32=True),
                        h, w_in, col + d_attn, d_attn, [F32, BF16], extra=(k_norm_g.reshape(1, HEAD_DIM),))
    v32, v16 = _in_proj(_plain_kernel, h, w_in, col + 2 * d_attn, d_attn, [F32, BF16])
    (gates,) = _in_proj(_gate_kernel, h, w_in, col + 3 * d_attn, 2 * d_model, [BF16])

    if paged is None:
        act_a = _conv_branch_prompt(a_glu, dw_a_w, dw_a_b, ln_a_g, ln_a_b)
        conv_state = a_glu[m - (dw_a_w.shape[0] - 1):].reshape(1, -1, d_conv)
        o = _attn_prompt(q16, k16, v16, sb_bias)
    else:
        a3 = a_glu.reshape(nb, t, d_conv)
        act_a = _conv_branch_sample(conv_hist, a3, dw_a_w, dw_a_b, ln_a_g, ln_a_b).reshape(m, d_conv)
        conv_state = jnp.concatenate([conv_hist, a3], axis=1)[:, t:]
        cache_k, cache_v, page_table = paged
        n_phys, page = cache_k.shape[:2]
        o = _attn_sample(q16.reshape(nb, t, d_attn), k32.reshape(nb, t, d_attn), v32.reshape(nb, t, d_attn),
                         cache_k.reshape(n_phys, page, d_attn), cache_v.reshape(n_phys, page, d_attn),
                         page_table, sb_bias).reshape(m, d_attn)

    merged = _merge(act_a, o, w_proj_a, w_proj_b, gates)
    x_mid = _residual_matmul(merged, w_out, x2d)

    h2 = _rmsnorm(x_mid, norm2_g)
    if paged is None:
        g, tails = _ffn_up(h2, w_up, dw_f_w, dw_f_b)
        ffn_state = tails[-1, SUBLANES - (dw_f_w.shape[0] - 1):].reshape(1, -1, d_ff)
    else:
        wf = dw_f_w.shape[0]
        hist_rows = jnp.pad(ffn_hist, ((0, 0), (0, t - (wf - 1)), (0, 0))).reshape(m, d_ff)
        g, tails = _ffn_up(h2, w_up, dw_f_w, dw_f_b, hist_rows=hist_rows, group=t)
        ffn_state = jnp.concatenate([ffn_hist, tails.reshape(nb, t, d_ff)], axis=1)[:, t:]
    y = _residual_matmul(g, w_down, x_mid)

    heads = (nb, t, d_attn // HEAD_DIM, HEAD_DIM)
    return (y.reshape(nb, t, d_model), k32.reshape(heads), v32.reshape(heads), conv_state, ffn_state)


def kernel(x_prompt, x_sample, cache_k, cache_v, state_conv, state_ffn_conv, page_table, norm1_g, w_in, dw_a_w, dw_a_b, ln_a_g, ln_a_b, w_proj_a, q_norm_g, k_norm_g, sb_bias, w_proj_b, w_out, norm2_g, w_up, dw_f_w, dw_f_b, w_down):
    assert x_prompt.shape[0] == 1
    to16 = lambda w: w.astype(BF16)
    wts = (norm1_g, to16(w_in), dw_a_w, dw_a_b, ln_a_g, ln_a_b, to16(w_proj_a), q_norm_g, k_norm_g, sb_bias,
           to16(w_proj_b), to16(w_out), norm2_g, to16(w_up), dw_f_w, dw_f_b, to16(w_down))
    yp, k_p, v_p, conv_p, ffn_p = _layer(x_prompt, wts)
    ys, k_s, v_s, conv_s, ffn_s = _layer(x_sample, wts, conv_hist=state_conv, ffn_hist=state_ffn_conv,
                                         paged=(cache_k, cache_v, page_table))
    return (yp, ys, k_p, v_p, conv_p, ffn_p, k_s, v_s, conv_s, ffn_s)
```

```python
import functools
import math

import jax
import jax.numpy as jnp
from jax import lax
from jax.experimental import pallas as pl
from jax.experimental.pallas import tpu as pltpu

HEAD_DIM = 128
RMS_EPS = 1e-6
LN_EPS = 1e-5
LOG2E = math.log2(math.e)
MASKED_LOG2 = -1e30
CONV_HALO = 32
SUBLANES = 8
V7X_VMEM_LIMIT = 56 * 1024 * 1024

F32 = jnp.float32
BF16 = jnp.bfloat16


def _largest_divisor(n, candidates):
    for c in candidates:
        if n % c == 0:
            return c
    raise ValueError(f"no tile in {candidates} divides {n}")


def _dot(a, b):
    return jnp.dot(a, b, preferred_element_type=F32)


def _sigmoid(x):
    return 1.0 / (1.0 + jnp.exp(-x))


def _params(*sem):
    return pltpu.CompilerParams(dimension_semantics=sem, vmem_limit_bytes=V7X_VMEM_LIMIT)


def _rmsnorm_kernel(x_ref, g_ref, o_ref):
    x = x_ref[...]
    ms = jnp.mean(x * x, axis=-1, keepdims=True)
    o_ref[...] = (x * lax.rsqrt(ms + RMS_EPS) * g_ref[...]).astype(o_ref.dtype)


def _rmsnorm(x, g):
    m, d = x.shape
    tr = _largest_divisor(m, (256, 128, 64, 32, 16, 8))
    return pl.pallas_call(
        _rmsnorm_kernel,
        grid=(m // tr,),
        in_specs=[pl.BlockSpec((tr, d), lambda i: (i, 0)),
                  pl.BlockSpec((1, d), lambda i: (0, 0))],
        out_specs=pl.BlockSpec((tr, d), lambda i: (i, 0)),
        out_shape=jax.ShapeDtypeStruct((m, d), BF16),
        compiler_params=_params("parallel"),
    )(x, g.reshape(1, d))


def _glu_kernel(h_ref, wa_ref, wb_ref, o_ref):
    h = h_ref[...]
    o_ref[...] = _dot(h, wa_ref[...]) * _sigmoid(_dot(h, wb_ref[...]))


def _headnorm_kernel(h_ref, w_ref, g_ref, *o_refs, scale, emit_f32):
    acc = _dot(h_ref[...], w_ref[...])
    g = g_ref[...]
    for hh in range(acc.shape[1] // HEAD_DIM):
        blk = acc[:, hh * HEAD_DIM:(hh + 1) * HEAD_DIM]
        ms = jnp.mean(blk * blk, axis=-1, keepdims=True)
        y = blk * lax.rsqrt(ms + RMS_EPS) * g
        cols = slice(hh * HEAD_DIM, (hh + 1) * HEAD_DIM)
        if emit_f32:
            o_refs[0][:, cols] = y
        o_refs[-1][:, cols] = (y * scale).astype(BF16)


def _plain_kernel(h_ref, w_ref, o32_ref, o16_ref):
    acc = _dot(h_ref[...], w_ref[...])
    o32_ref[...] = acc
    o16_ref[...] = acc.astype(BF16)


def _gate_kernel(h_ref, w_ref, o_ref):
    o_ref[...] = _sigmoid(_dot(h_ref[...], w_ref[...])).astype(BF16)


def _in_proj(kernel, h, w, col0, n_cols, out_dtypes, extra=(), n_w=1, w_stride=0):
    m, k = h.shape
    tm = _largest_divisor(m, (1024, 512, 256))
    tn = _largest_divisor(math.gcd(n_cols, col0, w_stride), (512, 256, 128))
    w_specs = [pl.BlockSpec((k, tn), functools.partial(
        lambda i, j, off: (0, j + off), off=(col0 + s * w_stride) // tn)) for s in range(n_w)]
    extra_specs = [pl.BlockSpec(e.shape, lambda i, j: (0, 0)) for e in extra]
    outs = pl.pallas_call(
        kernel,
        grid=(m // tm, n_cols // tn),
        in_specs=[pl.BlockSpec((tm, k), lambda i, j: (i, 0))] + w_specs + extra_specs,
        out_specs=[pl.BlockSpec((tm, tn), lambda i, j: (i, j)) for _ in out_dtypes],
        out_shape=[jax.ShapeDtypeStruct((m, n_cols), dt) for dt in out_dtypes],
        compiler_params=_params("parallel", "parallel"),
    )(h, *([w] * n_w), *extra)
    return outs


def _conv_ln_silu(xp_ref, conv_ref, w_ref, b_ref, g_ref, beta_ref, o_ref, *, rows, base):
    width, c = w_ref.shape
    lane_chunk = 128
    row_chunk = min(rows, 32)
    n_shift = SUBLANES

    def chunk(ci, carry):
        lanes = pl.ds(pl.multiple_of(ci * lane_chunk, lane_chunk), lane_chunk)
        wv = w_ref[:, lanes]
        bias = b_ref[:, lanes]
        for r0 in range(0, rows, row_chunk):
            acc = jnp.zeros((row_chunk, lane_chunk), F32) + bias
            for b in range(min(n_shift, width)):
                taps = range(b, width, n_shift)
                span = row_chunk + (len(taps) - 1) * n_shift
                shifted = xp_ref[pl.ds(base + r0 + b, span), lanes]
                for tap in taps:
                    acc = acc + shifted[tap - b:tap - b + row_chunk] * wv[tap:tap + 1]
            conv_ref[pl.ds(r0, row_chunk), lanes] = acc
        return carry

    lax.fori_loop(0, c // lane_chunk, chunk, 0)

    g = g_ref[...]
    beta = beta_ref[...]
    for r0 in range(0, rows, row_chunk):
        x = conv_ref[pl.ds(r0, row_chunk), :]
        mu = jnp.mean(x, axis=-1, keepdims=True)
        xc = x - mu
        var = jnp.mean(xc * xc, axis=-1, keepdims=True)
        y = xc * lax.rsqrt(var + LN_EPS) * g + beta
        o_ref[pl.ds(r0, row_chunk), :] = (y * _sigmoid(y)).astype(o_ref.dtype)


def _conv_prompt_kernel(halo_ref, x_ref, w_ref, b_ref, g_ref, beta_ref, o_ref, xp_ref, conv_ref, *, width):
    rows = x_ref.shape[0]
    first = pl.program_id(0) == 0
    xp_ref[pl.ds(0, CONV_HALO), :] = jnp.where(first, 0.0, halo_ref[...])
    xp_ref[pl.ds(CONV_HALO, rows), :] = x_ref[...]
    _conv_ln_silu(xp_ref, conv_ref, w_ref, b_ref, g_ref, beta_ref, o_ref,
                  rows=rows, base=CONV_HALO - (width - 1))


def _conv_sample_kernel(hist_ref, x_ref, w_ref, b_ref, g_ref, beta_ref, o_ref, xp_ref, conv_ref, *, width):
    rows = x_ref.shape[1]
    xp_ref[...] = jnp.zeros_like(xp_ref)
    xp_ref[pl.ds(0, width - 1), :] = hist_ref[0]
    xp_ref[pl.ds(width - 1, rows), :] = x_ref[0]
    _conv_ln_silu(xp_ref, conv_ref, w_ref, b_ref, g_ref, beta_ref, o_ref.at[0],
                  rows=rows, base=0)


def _conv_branch_prompt(a_glu, w, b, g, beta):
    t, c = a_glu.shape
    width = w.shape[0]
    assert width - 1 <= CONV_HALO
    tr = _largest_divisor(t, (256, 128, 64, 32))
    halo_blocks = tr // CONV_HALO
    vec = lambda v: v.reshape(1, c)
    return pl.pallas_call(
        functools.partial(_conv_prompt_kernel, width=width),
        grid=(t // tr,),
        in_specs=[pl.BlockSpec((CONV_HALO, c), lambda i: (jnp.maximum(i * halo_blocks - 1, 0), 0)),
                  pl.BlockSpec((tr, c), lambda i: (i, 0)),
                  pl.BlockSpec((width, c), lambda i: (0, 0))] +
                 [pl.BlockSpec((1, c), lambda i: (0, 0))] * 3,
        out_specs=pl.BlockSpec((tr, c), lambda i: (i, 0)),
        out_shape=jax.ShapeDtypeStruct((t, c), BF16),
        scratch_shapes=[pltpu.VMEM((CONV_HALO + tr, c), F32), pltpu.VMEM((tr, c), F32)],
        compiler_params=_params("parallel"),
    )(a_glu, a_glu, w, vec(b), vec(g), vec(beta))


def _conv_branch_sample(hist, a_glu, w, b, g, beta):
    nb, t, c = a_glu.shape
    width = w.shape[0]
    xp_rows = -(-(width - 1 + t + SUBLANES) // SUBLANES) * SUBLANES
    vec = lambda v: v.reshape(1, c)
    return pl.pallas_call(
        functools.partial(_conv_sample_kernel, width=width),
        grid=(nb,),
        in_specs=[pl.BlockSpec((1, width - 1, c), lambda i: (i, 0, 0)),
                  pl.BlockSpec((1, t, c), lambda i: (i, 0, 0)),
                  pl.BlockSpec((width, c), lambda i: (0, 0))] +
                 [pl.BlockSpec((1, c), lambda i: (0, 0))] * 3,
        out_specs=pl.BlockSpec((1, t, c), lambda i: (i, 0, 0)),
        out_shape=jax.ShapeDtypeStruct((nb, t, c), BF16),
        scratch_shapes=[pltpu.VMEM((xp_rows, c), F32), pltpu.VMEM((t, c), F32)],
        compiler_params=_params("parallel"),
    )(hist, a_glu, w, vec(b), vec(g), vec(beta))


def _log2_gates(z2):
    sign_bit = jnp.uint32(1 << 31)
    neg_abs = lax.bitcast_convert_type(lax.bitcast_convert_type(z2, jnp.uint32) | sign_bit, F32)
    lp = jnp.log(1.0 + jnp.exp2(neg_abs)) * LOG2E
    log_beta = jnp.minimum(z2, 0.0) - lp
    return log_beta, log_beta - z2


def _attn_prompt_kernel(bias_ref, q_ref, k_ref, v_ref, u_ref, o_ref, *, tk, n_heads):
    grp = pl.program_id(0)
    qi = pl.program_id(1)
    tq = q_ref.shape[0]
    assert tq == tk
    u = u_ref[...]

    heads = range(n_heads)

    def block_rows(kb):
        return pl.ds(pl.multiple_of(kb * tk, tk), tk)

    def gates(kb, mask):
        out = []
        for hh in heads:
            cols = slice(hh * HEAD_DIM, (hh + 1) * HEAD_DIM)
            z2 = lax.dot_general(q_ref[:, cols], k_ref[block_rows(kb), cols], (((1,), (1,)), ((), ())),
                                 preferred_element_type=F32) + bias_ref[grp * n_heads + hh]
            log_beta, log_1mb = _log2_gates(z2)
            if mask is not None:
                log_beta = jnp.where(mask, log_beta, MASKED_LOG2)
                log_1mb = jnp.where(mask, log_1mb, 0.0)
            out.append((log_beta, log_1mb.astype(BF16)))
        return tuple(out)

    def weigh(kb, pending, acc, run):
        tail = [_dot(pending[hh][1], u) for hh in heads]
        new_acc, new_run = [], []
        for hh in heads:
            log_beta, log_1mb = pending[hh]
            w = log_beta + tail[hh] + jnp.concatenate([run[hh]] * (tk // HEAD_DIM), axis=1)
            v = v_ref[block_rows(kb), hh * HEAD_DIM:(hh + 1) * HEAD_DIM]
            new_acc.append(acc[hh] + _dot(jnp.exp2(w).astype(BF16), v))
            total = tail[hh][:, :1] + log_1mb[:, :1].astype(F32)
            new_run.append(run[hh] + jnp.broadcast_to(total, (tq, HEAD_DIM)))
        return tuple(new_acc), tuple(new_run)

    def visit(kb, state, mask):
        return weigh(kb, gates(kb, mask), *state)

    causal = lax.broadcasted_iota(jnp.int32, (tq, tk), 1) < lax.broadcasted_iota(jnp.int32, (tq, tk), 0)
    zeros = tuple(jnp.zeros((tq, HEAD_DIM), F32) for _ in heads)
    state = visit(qi, (zeros, zeros), causal)
    acc, _ = lax.fori_loop(0, qi, lambda t, st: visit(qi - 1 - t, st, None), state)
    for hh in range(n_heads):
        o_ref[:, hh * HEAD_DIM:(hh + 1) * HEAD_DIM] = acc[hh].astype(o_ref.dtype)


def _attn_prompt(q, k, v, sb_bias2):
    t, d = q.shape
    nh = d // HEAD_DIM
    tq = _largest_divisor(t, (256, 128))
    hb = _largest_divisor(nh, (4, 2, 1))
    sum_mat = (jnp.arange(tq)[:, None] > jnp.arange(tq)[None, :]).astype(BF16)
    return pl.pallas_call(
        functools.partial(_attn_prompt_kernel, tk=tq, n_heads=hb),
        grid=(nh // hb, t // tq),
        in_specs=[pl.BlockSpec(memory_space=pltpu.SMEM),
                  pl.BlockSpec((tq, hb * HEAD_DIM), lambda h, i: (i, h)),
                  pl.BlockSpec((t, hb * HEAD_DIM), lambda h, i: (0, h)),
                  pl.BlockSpec((t, hb * HEAD_DIM), lambda h, i: (0, h)),
                  pl.BlockSpec(sum_mat.shape, lambda h, i: (0, 0))],
        out_specs=pl.BlockSpec((tq, hb * HEAD_DIM), lambda h, i: (i, h)),
        out_shape=jax.ShapeDtypeStruct((t, d), BF16),
        compiler_params=_params("parallel", "parallel"),
    )(sb_bias2, q, k, v, sum_mat)


def _attn_sample_kernel(pt_ref, qt_ref, bias_ref, knew_ref, vnew_ref, ut_ref, *rest, n_slots, t_new):
    k_refs = rest[:n_slots]
    v_refs = rest[n_slots:2 * n_slots]
    o_ref = rest[2 * n_slots]
    qbd_ref, run_ref, acc_ref, pad_k_ref, pad_v_ref = rest[2 * n_slots + 1:]
    p = pl.program_id(1)
    page, d = pad_k_ref.shape
    ncol = qbd_ref.shape[1]
    nh = d // HEAD_DIM
    bias = bias_ref[...]
    ut = ut_ref[...]

    def page_rows(ref):
        return jnp.concatenate([ref[0, pl.ds(hh, page, stride=nh), :] for hh in range(nh)],
                               axis=1).astype(BF16)

    def visit(k, v, mask):
        z2 = _dot(k, qbd_ref[...]) + bias
        log_beta, log_1mb = _log2_gates(z2)
        if mask is not None:
            log_1mb = jnp.where(mask, log_1mb, 0.0)
        tail = _dot(ut, log_1mb.astype(BF16))
        a = jnp.exp2(log_beta + tail + run_ref[...])
        if mask is not None:
            a = jnp.where(mask, a, 0.0)
        acc_ref[...] += lax.dot_general(a.astype(BF16), v, (((0,), (0,)), ((), ())),
                                        preferred_element_type=F32)
        run_ref[...] += jnp.sum(log_1mb, axis=0, keepdims=True)

    @pl.when(p == 0)
    def _():
        row_head = lax.broadcasted_iota(jnp.int32, (d, ncol), 0) // HEAD_DIM
        col_head = lax.broadcasted_iota(jnp.int32, (d, ncol), 1) // t_new
        qbd_ref[...] = jnp.where(row_head == col_head, qt_ref[0], jnp.zeros((), BF16))
        run_ref[...] = jnp.zeros_like(run_ref)
        acc_ref[...] = jnp.zeros_like(acc_ref)
        pad_k_ref[...] = jnp.zeros_like(pad_k_ref)
        pad_v_ref[...] = jnp.zeros_like(pad_v_ref)
        pad_k_ref[pl.ds(0, t_new), :] = knew_ref[0]
        pad_v_ref[pl.ds(0, t_new), :] = vnew_ref[0]
        key_t = lax.broadcasted_iota(jnp.int32, (page, ncol), 0)
        qry_t = lax.broadcasted_iota(jnp.int32, (page, ncol), 1) % t_new
        visit(pad_k_ref[...].astype(BF16), pad_v_ref[...].astype(BF16), key_t < qry_t)

    for s in range(n_slots):
        visit(page_rows(k_refs[s]), page_rows(v_refs[s]), None)

    @pl.when(p == pl.num_programs(1) - 1)
    def _():
        for hh in range(nh):
            o_ref[0, :, hh * HEAD_DIM:(hh + 1) * HEAD_DIM] = acc_ref[
                hh * t_new:(hh + 1) * t_new, hh * HEAD_DIM:(hh + 1) * HEAD_DIM].astype(o_ref.dtype)


def _attn_sample(q, k_new, v_new, cache_k, cache_v, page_table, sb_bias2):
    nb, t_new, d = q.shape
    nh = d // HEAD_DIM
    n_pages = page_table.shape[1]
    page = cache_k.shape[1] // nh
    ncol = nh * t_new
    n_slots = _largest_divisor(n_pages, (4, 2, 1))
    q_t = jnp.tile(jnp.swapaxes(q, 1, 2), (1, 1, nh))
    bias_cols = jnp.repeat(sb_bias2, t_new).reshape(1, ncol)
    tri_t = (jnp.arange(page)[None, :] > jnp.arange(page)[:, None]).astype(BF16)

    def page_spec(s):
        return pl.BlockSpec((1, page * nh, HEAD_DIM),
                            lambda b, p, pt: (pt[b, n_pages - 1 - p * n_slots - s], 0, 0))

    grid_spec = pltpu.PrefetchScalarGridSpec(
        num_scalar_prefetch=1,
        grid=(nb, n_pages // n_slots),
        in_specs=[pl.BlockSpec((1, d, ncol), lambda b, p, pt: (b, 0, 0)),
                  pl.BlockSpec((1, ncol), lambda b, p, pt: (0, 0)),
                  pl.BlockSpec((1, t_new, d), lambda b, p, pt: (b, 0, 0)),
                  pl.BlockSpec((1, t_new, d), lambda b, p, pt: (b, 0, 0)),
                  pl.BlockSpec((page, page), lambda b, p, pt: (0, 0))] +
                 [page_spec(s) for s in range(n_slots)] * 2,
        out_specs=pl.BlockSpec((1, t_new, d), lambda b, p, pt: (b, 0, 0)),
        scratch_shapes=[pltpu.VMEM((d, ncol), BF16), pltpu.VMEM((1, ncol), F32),
                        pltpu.VMEM((ncol, d), F32), pltpu.VMEM((page, d), F32),
                        pltpu.VMEM((page, d), F32)],
    )
    return pl.pallas_call(
        functools.partial(_attn_sample_kernel, n_slots=n_slots, t_new=t_new),
        grid_spec=grid_spec,
        out_shape=jax.ShapeDtypeStruct((nb, t_new, d), BF16),
        compiler_params=_params("parallel", "arbitrary"),
    )(page_table, q_t, bias_cols, k_new, v_new, tri_t,
      *([cache_k] * n_slots), *([cache_v] * n_slots))


def _merge_kernel(a_ref, o_ref, wa_ref, wb_ref, ga_ref, gb_ref, out_ref):
    ya = _dot(a_ref[...], wa_ref[...])
    yb = _dot(o_ref[...], wb_ref[...])
    out_ref[...] = (ga_ref[...].astype(F32) * ya + gb_ref[...].astype(F32) * yb).astype(out_ref.dtype)


def _merge(act_a, o, w_a, w_b, gates):
    m, ka = act_a.shape
    kb = o.shape[1]
    n = w_a.shape[1]
    tm = _largest_divisor(m, (1024, 512, 256))
    tn = _largest_divisor(n, (512, 256))
    return pl.pallas_call(
        _merge_kernel,
        grid=(m // tm, n // tn),
        in_specs=[pl.BlockSpec((tm, ka), lambda i, j: (i, 0)),
                  pl.BlockSpec((tm, kb), lambda i, j: (i, 0)),
                  pl.BlockSpec((ka, tn), lambda i, j: (0, j)),
                  pl.BlockSpec((kb, tn), lambda i, j: (0, j)),
                  pl.BlockSpec((tm, tn), lambda i, j: (i, j)),
                  pl.BlockSpec((tm, tn), lambda i, j: (i, j + n // tn))],
        out_specs=pl.BlockSpec((tm, tn), lambda i, j: (i, j)),
        out_shape=jax.ShapeDtypeStruct((m, n), BF16),
        compiler_params=_params("parallel", "parallel"),
    )(act_a, o, w_a, w_b, gates, gates)


def _residual_kernel(a_ref, w_ref, x_ref, o_ref):
    o_ref[...] = x_ref[...] + _dot(a_ref[...], w_ref[...])


def _residual_matmul(a, w, x):
    m, k = a.shape
    n = w.shape[1]
    tm = _largest_divisor(m, (512, 256))
    tn = _largest_divisor(n, (512, 256))
    return pl.pallas_call(
        _residual_kernel,
        grid=(m // tm, n // tn),
        in_specs=[pl.BlockSpec((tm, k), lambda i, j: (i, 0)),
                  pl.BlockSpec((k, tn), lambda i, j: (0, j)),
                  pl.BlockSpec((tm, tn), lambda i, j: (i, j))],
        out_specs=pl.BlockSpec((tm, tn), lambda i, j: (i, j)),
        out_shape=jax.ShapeDtypeStruct((m, n), F32),
        compiler_params=_params("parallel", "parallel"),
    )(a, w, x)


def _ffn_up_kernel(h_ref, wa_ref, wb_ref, cw_ref, cb_ref, *rest, group, tail_rows):
    if group is None:
        g_ref, tail_ref, carry_ref = rest
    else:
        hist_ref, g_ref, tail_ref = rest
    i = pl.program_id(0)
    j = pl.program_id(1)
    h = h_ref[...]
    f_a = _dot(h, wa_ref[...])
    f_b = _dot(h, wb_ref[...])
    tm = f_a.shape[0]
    width = cw_ref.shape[0]
    row = lax.broadcasted_iota(jnp.int32, f_a.shape, 0)
    conv = f_a * cw_ref[width - 1:width, :] + cb_ref[...]
    if group is None:
        @pl.when(i == 0)
        def _():
            carry_ref[j] = jnp.zeros(carry_ref.shape[1:], F32)

        prev_tail = carry_ref[j]
        for d in range(1, width):
            shifted = pltpu.roll(f_a, d, axis=0)
            for r in range(d):
                shifted = jnp.where(row == r, prev_tail[SUBLANES - d + r:SUBLANES - d + r + 1, :], shifted)
            conv = conv + shifted * cw_ref[width - 1 - d:width - d, :]
        carry_ref[j] = f_a[tm - SUBLANES:, :]
    else:
        hist = hist_ref[...]
        pos = row % group
        for d in range(1, width):
            shifted = jnp.where(pos < d, pltpu.roll(hist, (tm - (width - 1) + d) % tm, axis=0),
                                pltpu.roll(f_a, d, axis=0))
            conv = conv + shifted * cw_ref[width - 1 - d:width - d, :]
    g_ref[...] = (conv * _sigmoid(conv) * f_b).astype(g_ref.dtype)
    tail_ref[0] = f_a[tm - tail_rows:, :]


def _ffn_up(h2, w_up, conv_w, conv_b, hist_rows=None, group=None):
    m, k = h2.shape
    f = w_up.shape[1] // 2
    width = conv_w.shape[0]
    tm = _largest_divisor(m, (1024, 512, 256))
    tn = _largest_divisor(f, (256, 128))
    nj = f // tn
    tail_rows = SUBLANES if group is None else tm
    assert width - 1 <= SUBLANES
    in_specs = [pl.BlockSpec((tm, k), lambda i, j: (i, 0)),
                pl.BlockSpec((k, tn), lambda i, j: (0, j)),
                pl.BlockSpec((k, tn), lambda i, j: (0, j + nj)),
                pl.BlockSpec((width, tn), lambda i, j: (0, j)),
                pl.BlockSpec((1, tn), lambda i, j: (0, j))]
    args = [h2, w_up, w_up, conv_w, conv_b.reshape(1, f)]
    scratch = []
    if group is None:
        scratch = [pltpu.VMEM((nj, SUBLANES, tn), F32)]
    else:
        assert m == tm
        in_specs.append(pl.BlockSpec((tm, tn), lambda i, j: (i, j)))
        args.append(hist_rows)
    return pl.pallas_call(
        functools.partial(_ffn_up_kernel, group=group, tail_rows=tail_rows),
        grid=(m // tm, nj),
        in_specs=in_specs,
        out_specs=[pl.BlockSpec((tm, tn), lambda i, j: (i, j)),
                   pl.BlockSpec((1, tail_rows, tn), lambda i, j: (i, 0, j))],
        out_shape=[jax.ShapeDtypeStruct((m, f), BF16),
                   jax.ShapeDtypeStruct((m // tm, tail_rows, f), F32)],
        scratch_shapes=scratch,
        compiler_params=_params("arbitrary", "arbitrary"),
    )(*args)


def _layer(x, wts, *, conv_hist=None, ffn_hist=None, paged=None):
    (norm1_g, w_in, dw_a_w, dw_a_b, ln_a_g, ln_a_b, w_proj_a, q_norm_g, k_norm_g, sb_bias,
     w_proj_b, w_out, norm2_g, w_up, dw_f_w, dw_f_b, w_down) = wts
    nb, t, d_model = x.shape
    m = nb * t
    d_conv = dw_a_w.shape[1]
    d_attn = w_proj_b.shape[0]
    d_ff = dw_f_w.shape[1]
    x2d = x.reshape(m, d_model)

    h = _rmsnorm(x2d, norm1_g)
    (a_glu,) = _in_proj(_glu_kernel, h, w_in, 0, d_conv, [F32], n_w=2, w_stride=d_conv)
    col = 2 * d_conv
    sb_bias2 = sb_bias * LOG2E
    (q16,) = _in_proj(functools.partial(_headnorm_kernel, scale=HEAD_DIM ** -0.5 * LOG2E, emit_f32=False),
                      h, w_in, col, d_attn, [BF16], extra=(q_norm_g.reshape(1, HEAD_DIM),))
    k32, k16 = _in_proj(functools.partial(_headnorm_kernel, scale=1.0, emit_f32=True),
                        h, w_in, col + d_attn, d_attn, [F32, BF16], extra=(k_norm_g.reshape(1, HEAD_DIM),))
    v32, v16 = _in_proj(_plain_kernel, h, w_in, col + 2 * d_attn, d_attn, [F32, BF16])
    (gates,) = _in_proj(_gate_kernel, h, w_in, col + 3 * d_attn, 2 * d_model, [BF16])

    if paged is None:
        act_a = _conv_branch_prompt(a_glu, dw_a_w, dw_a_b, ln_a_g, ln_a_b)
        conv_state = a_glu[m - (dw_a_w.shape[0] - 1):].reshape(1, -1, d_conv)
        o = _attn_prompt(q16, k16, v16, sb_bias2)
    else:
        a3 = a_glu.reshape(nb, t, d_conv)
        act_a = _conv_branch_sample(conv_hist, a3, dw_a_w, dw_a_b, ln_a_g, ln_a_b).reshape(m, d_conv)
        conv_state = jnp.concatenate([conv_hist, a3], axis=1)[:, t:]
        cache_k, cache_v, page_table = paged
        n_phys, page, nh = cache_k.shape[:3]
        o = _attn_sample(q16.reshape(nb, t, d_attn), k32.reshape(nb, t, d_attn), v32.reshape(nb, t, d_attn),
                         cache_k.reshape(n_phys, page * nh, HEAD_DIM), cache_v.reshape(n_phys, page * nh, HEAD_DIM),
                         page_table, sb_bias2).reshape(m, d_attn)

    merged = _merge(act_a, o, w_proj_a, w_proj_b, gates)
    x_mid = _residual_matmul(merged, w_out, x2d)

    h2 = _rmsnorm(x_mid, norm2_g)
    if paged is None:
        g, tails = _ffn_up(h2, w_up, dw_f_w, dw_f_b)
        ffn_state = tails[-1, SUBLANES - (dw_f_w.shape[0] - 1):].reshape(1, -1, d_ff)
    else:
        wf = dw_f_w.shape[0]
        hist_rows = jnp.pad(ffn_hist, ((0, 0), (0, t - (wf - 1)), (0, 0))).reshape(m, d_ff)
        g, tails = _ffn_up(h2, w_up, dw_f_w, dw_f_b, hist_rows=hist_rows, group=t)
        ffn_state = jnp.concatenate([ffn_hist, tails.reshape(nb, t, d_ff)], axis=1)[:, t:]
    y = _residual_matmul(g, w_down, x_mid)

    heads = (nb, t, d_attn // HEAD_DIM, HEAD_DIM)
    return (y.reshape(nb, t, d_model), k32.reshape(heads), v32.reshape(heads), conv_state, ffn_state)


def kernel(x_prompt, x_sample, cache_k, cache_v, state_conv, state_ffn_conv, page_table, norm1_g, w_in, dw_a_w, dw_a_b, ln_a_g, ln_a_b, w_proj_a, q_norm_g, k_norm_g, sb_bias, w_proj_b, w_out, norm2_g, w_up, dw_f_w, dw_f_b, w_down):
    assert x_prompt.shape[0] == 1
    to16 = lambda w: w.astype(BF16)
    wts = (norm1_g, to16(w_in), dw_a_w, dw_a_b, ln_a_g, ln_a_b, to16(w_proj_a), q_norm_g, k_norm_g, sb_bias,
           to16(w_proj_b), to16(w_out), norm2_g, to16(w_up), dw_f_w, dw_f_b, to16(w_down))
    yp, k_p, v_p, conv_p, ffn_p = _layer(x_prompt, wts)
    ys, k_s, v_s, conv_s, ffn_s = _layer(x_sample, wts, conv_hist=state_conv, ffn_hist=state_ffn_conv,
                                         paged=(cache_k, cache_v, page_table))
    return (yp, ys, k_p, v_p, conv_p, ffn_p, k_s, v_s, conv_s, ffn_s)
```

```python
import functools
import math

import jax
import jax.numpy as jnp
from jax import lax
from jax.experimental import pallas as pl
from jax.experimental.pallas import tpu as pltpu

HEAD_DIM = 128
RMS_EPS = 1e-6
LN_EPS = 1e-5
LOG2E = math.log2(math.e)
MASKED_LOG2 = -1e30
CONV_HALO = 32
SUBLANES = 8
HEAD_ROW_PAD = 4
V7X_VMEM_LIMIT = 56 * 1024 * 1024

F32 = jnp.float32
BF16 = jnp.bfloat16


def _largest_divisor(n, candidates):
    for c in candidates:
        if n % c == 0:
            return c
    raise ValueError(f"no tile in {candidates} divides {n}")


def _dot(a, b):
    return jnp.dot(a, b, preferred_element_type=F32)


def _matmul(a, w_ref):
    return jnp.dot(a, w_ref[...].astype(BF16), preferred_element_type=F32)


def _sigmoid(x):
    return 1.0 / (1.0 + jnp.exp(-x))


def _params(*sem):
    return pltpu.CompilerParams(dimension_semantics=sem, vmem_limit_bytes=V7X_VMEM_LIMIT)


def _rmsnorm_kernel(x_ref, g_ref, o_ref):
    x = x_ref[...]
    ms = jnp.mean(x * x, axis=-1, keepdims=True)
    o_ref[...] = (x * lax.rsqrt(ms + RMS_EPS) * g_ref[...]).astype(o_ref.dtype)


def _rmsnorm(x, g):
    m, d = x.shape
    tr = _largest_divisor(m, (256, 128, 64, 32, 16, 8))
    return pl.pallas_call(
        _rmsnorm_kernel,
        grid=(m // tr,),
        in_specs=[pl.BlockSpec((tr, d), lambda i: (i, 0)),
                  pl.BlockSpec((1, d), lambda i: (0, 0))],
        out_specs=pl.BlockSpec((tr, d), lambda i: (i, 0)),
        out_shape=jax.ShapeDtypeStruct((m, d), BF16),
        compiler_params=_params("parallel"),
    )(x, g.reshape(1, d))


def _glu_kernel(h_ref, wa_ref, wb_ref, o_ref):
    h = h_ref[...]
    o_ref[...] = _matmul(h, wa_ref) * _sigmoid(_matmul(h, wb_ref))


def _headnorm_kernel(h_ref, w_ref, g_ref, *o_refs, scale, emit_f32):
    acc = _matmul(h_ref[...], w_ref)
    g = g_ref[...]
    for hh in range(acc.shape[1] // HEAD_DIM):
        blk = acc[:, hh * HEAD_DIM:(hh + 1) * HEAD_DIM]
        ms = jnp.mean(blk * blk, axis=-1, keepdims=True)
        y = blk * lax.rsqrt(ms + RMS_EPS) * g
        cols = slice(hh * HEAD_DIM, (hh + 1) * HEAD_DIM)
        if emit_f32:
            o_refs[0][:, cols] = y
        o_refs[-1][:, cols] = (y * scale).astype(BF16)


def _plain_kernel(h_ref, w_ref, o32_ref, o16_ref):
    acc = _matmul(h_ref[...], w_ref)
    o32_ref[...] = acc
    o16_ref[...] = acc.astype(BF16)


def _gate_kernel(h_ref, w_ref, o_ref):
    o_ref[...] = _sigmoid(_matmul(h_ref[...], w_ref)).astype(BF16)


def _in_proj(kernel, h, w, col0, n_cols, out_dtypes, extra=(), n_w=1, w_stride=0):
    m, k = h.shape
    tm = _largest_divisor(m, (1024, 512, 256))
    tn = _largest_divisor(math.gcd(n_cols, col0, w_stride), (512, 256, 128) if n_w == 1 else (256, 128))
    w_specs = [pl.BlockSpec((k, tn), functools.partial(
        lambda i, j, off: (0, j + off), off=(col0 + s * w_stride) // tn)) for s in range(n_w)]
    extra_specs = [pl.BlockSpec(e.shape, lambda i, j: (0, 0)) for e in extra]
    outs = pl.pallas_call(
        kernel,
        grid=(m // tm, n_cols // tn),
        in_specs=[pl.BlockSpec((tm, k), lambda i, j: (i, 0))] + w_specs + extra_specs,
        out_specs=[pl.BlockSpec((tm, tn), lambda i, j: (i, j)) for _ in out_dtypes],
        out_shape=[jax.ShapeDtypeStruct((m, n_cols), dt) for dt in out_dtypes],
        compiler_params=_params("parallel", "parallel"),
    )(h, *([w] * n_w), *extra)
    return outs


def _conv_ln_silu(xp_ref, conv_ref, w_ref, b_ref, g_ref, beta_ref, o_ref, *, rows, base):
    width, c = w_ref.shape
    lane_chunk = 128
    row_chunk = min(rows, 32)
    n_shift = SUBLANES

    def chunk(ci, carry):
        lanes = pl.ds(pl.multiple_of(ci * lane_chunk, lane_chunk), lane_chunk)
        wv = w_ref[:, lanes]
        bias = b_ref[:, lanes]
        for r0 in range(0, rows, row_chunk):
            acc = jnp.zeros((row_chunk, lane_chunk), F32) + bias
            for b in range(min(n_shift, width)):
                taps = range(b, width, n_shift)
                span = row_chunk + (len(taps) - 1) * n_shift
                shifted = xp_ref[pl.ds(base + r0 + b, span), lanes]
                for tap in taps:
                    acc = acc + shifted[tap - b:tap - b + row_chunk] * wv[tap:tap + 1]
            conv_ref[pl.ds(r0, row_chunk), lanes] = acc
        return carry

    lax.fori_loop(0, c // lane_chunk, chunk, 0)

    g = g_ref[...]
    beta = beta_ref[...]
    for r0 in range(0, rows, row_chunk):
        x = conv_ref[pl.ds(r0, row_chunk), :]
        mu = jnp.mean(x, axis=-1, keepdims=True)
        xc = x - mu
        var = jnp.mean(xc * xc, axis=-1, keepdims=True)
        y = xc * lax.rsqrt(var + LN_EPS) * g + beta
        o_ref[pl.ds(r0, row_chunk), :] = (y * _sigmoid(y)).astype(o_ref.dtype)


def _conv_prompt_kernel(halo_ref, x_ref, w_ref, b_ref, g_ref, beta_ref, o_ref, xp_ref, conv_ref, *, width):
    rows = x_ref.shape[0]
    first = pl.program_id(0) == 0
    xp_ref[pl.ds(0, CONV_HALO), :] = jnp.where(first, 0.0, halo_ref[...])
    xp_ref[pl.ds(CONV_HALO, rows), :] = x_ref[...]
    _conv_ln_silu(xp_ref, conv_ref, w_ref, b_ref, g_ref, beta_ref, o_ref,
                  rows=rows, base=CONV_HALO - (width - 1))


def _conv_sample_kernel(hist_ref, x_ref, w_ref, b_ref, g_ref, beta_ref, o_ref, xp_ref, conv_ref, *, width):
    rows = x_ref.shape[1]
    xp_ref[...] = jnp.zeros_like(xp_ref)
    xp_ref[pl.ds(0, width - 1), :] = hist_ref[0]
    xp_ref[pl.ds(width - 1, rows), :] = x_ref[0]
    _conv_ln_silu(xp_ref, conv_ref, w_ref, b_ref, g_ref, beta_ref, o_ref.at[0],
                  rows=rows, base=0)


def _conv_branch_prompt(a_glu, w, b, g, beta):
    t, c = a_glu.shape
    width = w.shape[0]
    assert width - 1 <= CONV_HALO
    tr = _largest_divisor(t, (256, 128, 64, 32))
    halo_blocks = tr // CONV_HALO
    vec = lambda v: v.reshape(1, c)
    return pl.pallas_call(
        functools.partial(_conv_prompt_kernel, width=width),
        grid=(t // tr,),
        in_specs=[pl.BlockSpec((CONV_HALO, c), lambda i: (jnp.maximum(i * halo_blocks - 1, 0), 0)),
                  pl.BlockSpec((tr, c), lambda i: (i, 0)),
                  pl.BlockSpec((width, c), lambda i: (0, 0))] +
                 [pl.BlockSpec((1, c), lambda i: (0, 0))] * 3,
        out_specs=pl.BlockSpec((tr, c), lambda i: (i, 0)),
        out_shape=jax.ShapeDtypeStruct((t, c), BF16),
        scratch_shapes=[pltpu.VMEM((CONV_HALO + tr, c), F32), pltpu.VMEM((tr, c), F32)],
        compiler_params=_params("parallel"),
    )(a_glu, a_glu, w, vec(b), vec(g), vec(beta))


def _conv_branch_sample(hist, a_glu, w, b, g, beta):
    nb, t, c = a_glu.shape
    width = w.shape[0]
    xp_rows = -(-(width - 1 + t + SUBLANES) // SUBLANES) * SUBLANES
    vec = lambda v: v.reshape(1, c)
    return pl.pallas_call(
        functools.partial(_conv_sample_kernel, width=width),
        grid=(nb,),
        in_specs=[pl.BlockSpec((1, width - 1, c), lambda i: (i, 0, 0)),
                  pl.BlockSpec((1, t, c), lambda i: (i, 0, 0)),
                  pl.BlockSpec((width, c), lambda i: (0, 0))] +
                 [pl.BlockSpec((1, c), lambda i: (0, 0))] * 3,
        out_specs=pl.BlockSpec((1, t, c), lambda i: (i, 0, 0)),
        out_shape=jax.ShapeDtypeStruct((nb, t, c), BF16),
        scratch_shapes=[pltpu.VMEM((xp_rows, c), F32), pltpu.VMEM((t, c), F32)],
        compiler_params=_params("parallel"),
    )(hist, a_glu, w, vec(b), vec(g), vec(beta))


def _log2_gates(z2):
    sign_bit = jnp.uint32(1 << 31)
    neg_abs = lax.bitcast_convert_type(lax.bitcast_convert_type(z2, jnp.uint32) | sign_bit, F32)
    lp = jnp.log(1.0 + jnp.exp2(neg_abs)) * LOG2E
    log_beta = jnp.minimum(z2, 0.0) - lp
    return log_beta, log_beta - z2


def _attn_prompt_kernel(bias_ref, q_ref, k_ref, v_ref, u_ref, o_ref, *, tk, n_heads):
    grp = pl.program_id(0)
    qi = pl.program_id(1)
    tq = q_ref.shape[0]
    u = u_ref[...]
    sub = u.shape[0]
    n_sub = tk // sub

    heads = range(n_heads)

    def block_rows(kb):
        return pl.ds(pl.multiple_of(kb * tk, tk), tk)

    def gates(kb, mask):
        out = []
        for hh in heads:
            cols = slice(hh * HEAD_DIM, (hh + 1) * HEAD_DIM)
            z2 = lax.dot_general(q_ref[:, cols], k_ref[block_rows(kb), cols], (((1,), (1,)), ((), ())),
                                 preferred_element_type=F32) + bias_ref[grp * n_heads + hh]
            log_beta, log_1mb = _log2_gates(z2)
            if mask is not None:
                log_beta = jnp.where(mask, log_beta, MASKED_LOG2)
                log_1mb = jnp.where(mask, log_1mb, 0.0)
            out.append((log_beta, log_1mb.astype(BF16)))
        return tuple(out)

    def weigh(kb, pending, acc, run):
        tails = [[_dot(pending[hh][1][:, sb * sub:(sb + 1) * sub], u) for sb in range(n_sub)] for hh in heads]
        new_acc, new_run = [], []
        for hh in heads:
            log_beta, log_1mb = pending[hh]
            r = run[hh]
            weights = [None] * n_sub
            for sb in reversed(range(n_sub)):
                t = tails[hh][sb]
                w = log_beta[:, sb * sub:(sb + 1) * sub] + t + jnp.concatenate([r] * (sub // HEAD_DIM), axis=1)
                weights[sb] = jnp.exp2(w).astype(BF16)
                total = t[:, :1] + log_1mb[:, sb * sub:sb * sub + 1].astype(F32)
                r = r + jnp.broadcast_to(total, (tq, HEAD_DIM))
            v = v_ref[block_rows(kb), hh * HEAD_DIM:(hh + 1) * HEAD_DIM]
            new_acc.append(acc[hh] + _dot(jnp.concatenate(weights, axis=1), v))
            new_run.append(r)
        return tuple(new_acc), tuple(new_run)

    def visit(kb, state, mask):
        return weigh(kb, gates(kb, mask), *state)

    first_row = qi * tq
    diag = first_row // tk
    causal = (lax.broadcasted_iota(jnp.int32, (tq, tk), 1)
              < lax.broadcasted_iota(jnp.int32, (tq, tk), 0) + (first_row - diag * tk))
    zeros = tuple(jnp.zeros((tq, HEAD_DIM), F32) for _ in heads)
    state = visit(diag, (zeros, zeros), causal)
    acc, _ = lax.fori_loop(0, diag, lambda t, st: visit(diag - 1 - t, st, None), state)
    for hh in range(n_heads):
        o_ref[:, hh * HEAD_DIM:(hh + 1) * HEAD_DIM] = acc[hh].astype(o_ref.dtype)


def _attn_prompt(q, k, v, sb_bias2):
    t, d = q.shape
    nh = d // HEAD_DIM
    tq = _largest_divisor(t, (256, 128))
    tk = _largest_divisor(t, (2 * tq, tq))
    hb = _largest_divisor(nh, (4, 2, 1))
    sum_mat = (jnp.arange(tq)[:, None] > jnp.arange(tq)[None, :]).astype(BF16)
    return pl.pallas_call(
        functools.partial(_attn_prompt_kernel, tk=tk, n_heads=hb),
        grid=(nh // hb, t // tq),
        in_specs=[pl.BlockSpec(memory_space=pltpu.SMEM),
                  pl.BlockSpec((tq, hb * HEAD_DIM), lambda h, i: (i, h)),
                  pl.BlockSpec((t, hb * HEAD_DIM), lambda h, i: (0, h)),
                  pl.BlockSpec((t, hb * HEAD_DIM), lambda h, i: (0, h)),
                  pl.BlockSpec(sum_mat.shape, lambda h, i: (0, 0))],
        out_specs=pl.BlockSpec((tq, hb * HEAD_DIM), lambda h, i: (i, h)),
        out_shape=jax.ShapeDtypeStruct((t, d), BF16),
        compiler_params=_params("parallel", "parallel"),
    )(sb_bias2, q, k, v, sum_mat)


def _attn_sample_kernel(pt_ref, qt_ref, bias_ref, knew_ref, vnew_ref, ut_ref, *rest, n_slots, t_new):
    k_refs = rest[:n_slots]
    v_refs = rest[n_slots:2 * n_slots]
    o_ref = rest[2 * n_slots]
    qbd_ref, run_ref, acc_ref, pad_k_ref, pad_v_ref = rest[2 * n_slots + 1:2 * n_slots + 6]
    by_head_refs = rest[2 * n_slots + 6:]
    p = pl.program_id(1)
    page, d = pad_k_ref.shape
    ncol = qbd_ref.shape[1]
    nh = d // HEAD_DIM
    bias = bias_ref[...]
    ut = ut_ref[...]

    head_group = min(nh, SUBLANES)
    pitch = by_head_refs[0].shape[0] // nh

    def page_rows(ref, by_head_ref):
        for key in range(page):
            for h0 in range(0, nh, head_group):
                by_head_ref[pl.ds(h0 * pitch + key, head_group, stride=pitch), :] = (
                    ref[0, pl.ds(key * nh + h0, head_group), :])
        return jnp.concatenate([by_head_ref[pl.ds(hh * pitch, page), :] for hh in range(nh)],
                               axis=1).astype(BF16)

    def visit(ks, vs, mask):
        gates = [_log2_gates(_dot(k, qbd_ref[...]) + bias) for k in ks]
        log_1mb = [g[1] if mask is None else jnp.where(mask, g[1], 0.0) for g in gates]
        tails = [_dot(ut, l.astype(BF16)) for l in log_1mb]
        run = run_ref[...]
        weights = []
        for g, l, tail in zip(gates, log_1mb, tails):
            a = jnp.exp2(g[0] + tail + run)
            weights.append((a if mask is None else jnp.where(mask, a, 0.0)).astype(BF16))
            run = run + jnp.sum(l, axis=0, keepdims=True)
        acc_ref[...] += lax.dot_general(jnp.concatenate(weights, axis=0), jnp.concatenate(vs, axis=0),
                                        (((0,), (0,)), ((), ())), preferred_element_type=F32)
        run_ref[...] = run

    @pl.when(p == 0)
    def _():
        row_head = lax.broadcasted_iota(jnp.int32, (d, ncol), 0) // HEAD_DIM
        col_head = lax.broadcasted_iota(jnp.int32, (d, ncol), 1) // t_new
        qbd_ref[...] = jnp.where(row_head == col_head, qt_ref[0], jnp.zeros((), BF16))
        run_ref[...] = jnp.zeros_like(run_ref)
        acc_ref[...] = jnp.zeros_like(acc_ref)
        pad_k_ref[...] = jnp.zeros_like(pad_k_ref)
        pad_v_ref[...] = jnp.zeros_like(pad_v_ref)
        pad_k_ref[pl.ds(0, t_new), :] = knew_ref[0]
        pad_v_ref[pl.ds(0, t_new), :] = vnew_ref[0]
        key_t = lax.broadcasted_iota(jnp.int32, (page, ncol), 0)
        qry_t = lax.broadcasted_iota(jnp.int32, (page, ncol), 1) % t_new
        visit([pad_k_ref[...].astype(BF16)], [pad_v_ref[...].astype(BF16)], key_t < qry_t)

    visit([page_rows(k_refs[s], by_head_refs[2 * s]) for s in range(n_slots)],
          [page_rows(v_refs[s], by_head_refs[2 * s + 1]) for s in range(n_slots)], None)

    @pl.when(p == pl.num_programs(1) - 1)
    def _():
        for hh in range(nh):
            o_ref[0, :, hh * HEAD_DIM:(hh + 1) * HEAD_DIM] = acc_ref[
                hh * t_new:(hh + 1) * t_new, hh * HEAD_DIM:(hh + 1) * HEAD_DIM].astype(o_ref.dtype)


def _attn_sample(q, k_new, v_new, cache_k, cache_v, page_table, sb_bias2):
    nb, t_new, d = q.shape
    nh = d // HEAD_DIM
    n_pages = page_table.shape[1]
    page = cache_k.shape[1] // nh
    ncol = nh * t_new
    n_slots = _largest_divisor(n_pages, (4, 2, 1))
    q_t = jnp.tile(jnp.swapaxes(q, 1, 2), (1, 1, nh))
    bias_cols = jnp.repeat(sb_bias2, t_new).reshape(1, ncol)
    tri_t = (jnp.arange(page)[None, :] > jnp.arange(page)[:, None]).astype(BF16)

    def page_spec(s):
        return pl.BlockSpec((1, page * nh, HEAD_DIM),
                            lambda b, p, pt: (pt[b, n_pages - 1 - p * n_slots - s], 0, 0))

    grid_spec = pltpu.PrefetchScalarGridSpec(
        num_scalar_prefetch=1,
        grid=(nb, n_pages // n_slots),
        in_specs=[pl.BlockSpec((1, d, ncol), lambda b, p, pt: (b, 0, 0)),
                  pl.BlockSpec((1, ncol), lambda b, p, pt: (0, 0)),
                  pl.BlockSpec((1, t_new, d), lambda b, p, pt: (b, 0, 0)),
                  pl.BlockSpec((1, t_new, d), lambda b, p, pt: (b, 0, 0)),
                  pl.BlockSpec((page, page), lambda b, p, pt: (0, 0))] +
                 [page_spec(s) for s in range(n_slots)] * 2,
        out_specs=pl.BlockSpec((1, t_new, d), lambda b, p, pt: (b, 0, 0)),
        scratch_shapes=[pltpu.VMEM((d, ncol), BF16), pltpu.VMEM((1, ncol), F32),
                        pltpu.VMEM((ncol, d), F32), pltpu.VMEM((page, d), F32),
                        pltpu.VMEM((page, d), F32)] +
                       [pltpu.VMEM((nh * (page + HEAD_ROW_PAD), HEAD_DIM), F32)] * (2 * n_slots),
    )
    return pl.pallas_call(
        functools.partial(_attn_sample_kernel, n_slots=n_slots, t_new=t_new),
        grid_spec=grid_spec,
        out_shape=jax.ShapeDtypeStruct((nb, t_new, d), BF16),
        compiler_params=_params("parallel", "arbitrary"),
    )(page_table, q_t, bias_cols, k_new, v_new, tri_t,
      *([cache_k] * n_slots), *([cache_v] * n_slots))


def _merge_kernel(a_ref, o_ref, wa_ref, wb_ref, ga_ref, gb_ref, out_ref):
    ya = _matmul(a_ref[...], wa_ref)
    yb = _matmul(o_ref[...], wb_ref)
    out_ref[...] = (ga_ref[...].astype(F32) * ya + gb_ref[...].astype(F32) * yb).astype(out_ref.dtype)


def _merge(act_a, o, w_a, w_b, gates):
    m, ka = act_a.shape
    kb = o.shape[1]
    n = w_a.shape[1]
    tm = _largest_divisor(m, (1024, 512, 256))
    tn = _largest_divisor(n, (512, 256))
    return pl.pallas_call(
        _merge_kernel,
        grid=(m // tm, n // tn),
        in_specs=[pl.BlockSpec((tm, ka), lambda i, j: (i, 0)),
                  pl.BlockSpec((tm, kb), lambda i, j: (i, 0)),
                  pl.BlockSpec((ka, tn), lambda i, j: (0, j)),
                  pl.BlockSpec((kb, tn), lambda i, j: (0, j)),
                  pl.BlockSpec((tm, tn), lambda i, j: (i, j)),
                  pl.BlockSpec((tm, tn), lambda i, j: (i, j + n // tn))],
        out_specs=pl.BlockSpec((tm, tn), lambda i, j: (i, j)),
        out_shape=jax.ShapeDtypeStruct((m, n), BF16),
        compiler_params=_params("parallel", "parallel"),
    )(act_a, o, w_a, w_b, gates, gates)


def _residual_kernel(a_ref, w_ref, x_ref, o_ref):
    o_ref[...] = x_ref[...] + _matmul(a_ref[...], w_ref)


def _residual_matmul(a, w, x):
    m, k = a.shape
    n = w.shape[1]
    tm = _largest_divisor(m, (1024, 512, 256) if k <= 4096 else (512, 256))
    tn = _largest_divisor(n, (512, 256))
    return pl.pallas_call(
        _residual_kernel,
        grid=(m // tm, n // tn),
        in_specs=[pl.BlockSpec((tm, k), lambda i, j: (i, 0)),
                  pl.BlockSpec((k, tn), lambda i, j: (0, j)),
                  pl.BlockSpec((tm, tn), lambda i, j: (i, j))],
        out_specs=pl.BlockSpec((tm, tn), lambda i, j: (i, j)),
        out_shape=jax.ShapeDtypeStruct((m, n), F32),
        compiler_params=_params("parallel", "parallel"),
    )(a, w, x)


def _ffn_up_kernel(h_ref, wa_ref, wb_ref, cw_ref, cb_ref, *rest, group, tail_rows):
    if group is None:
        g_ref, tail_ref, carry_ref = rest
    else:
        hist_ref, g_ref, tail_ref = rest
    i = pl.program_id(0)
    j = pl.program_id(1)
    h = h_ref[...]
    f_a = _matmul(h, wa_ref)
    f_b = _matmul(h, wb_ref)
    tm = f_a.shape[0]
    width = cw_ref.shape[0]
    row = lax.broadcasted_iota(jnp.int32, f_a.shape, 0)
    conv = f_a * cw_ref[width - 1:width, :] + cb_ref[...]
    if group is None:
        @pl.when(i == 0)
        def _():
            carry_ref[j] = jnp.zeros(carry_ref.shape[1:], F32)

        prev_tail = carry_ref[j]
        for d in range(1, width):
            shifted = pltpu.roll(f_a, d, axis=0)
            for r in range(d):
                shifted = jnp.where(row == r, prev_tail[SUBLANES - d + r:SUBLANES - d + r + 1, :], shifted)
            conv = conv + shifted * cw_ref[width - 1 - d:width - d, :]
        carry_ref[j] = f_a[tm - SUBLANES:, :]
    else:
        hist = hist_ref[...]
        pos = row % group
        for d in range(1, width):
            shifted = jnp.where(pos < d, pltpu.roll(hist, (tm - (width - 1) + d) % tm, axis=0),
                                pltpu.roll(f_a, d, axis=0))
            conv = conv + shifted * cw_ref[width - 1 - d:width - d, :]
    g_ref[...] = (conv * _sigmoid(conv) * f_b).astype(g_ref.dtype)
    tail_ref[0] = f_a[tm - tail_rows:, :]


def _ffn_up(h2, w_up, conv_w, conv_b, hist_rows=None, group=None):
    m, k = h2.shape
    f = w_up.shape[1] // 2
    width = conv_w.shape[0]
    tm = _largest_divisor(m, (1024, 512, 256))
    tn = _largest_divisor(f, (256, 128))
    nj = f // tn
    tail_rows = SUBLANES if group is None else tm
    assert width - 1 <= SUBLANES
    in_specs = [pl.BlockSpec((tm, k), lambda i, j: (i, 0)),
                pl.BlockSpec((k, tn), lambda i, j: (0, j)),
                pl.BlockSpec((k, tn), lambda i, j: (0, j + nj)),
                pl.BlockSpec((width, tn), lambda i, j: (0, j)),
                pl.BlockSpec((1, tn), lambda i, j: (0, j))]
    args = [h2, w_up, w_up, conv_w, conv_b.reshape(1, f)]
    scratch = []
    if group is None:
        scratch = [pltpu.VMEM((nj, SUBLANES, tn), F32)]
    else:
        assert m == tm
        in_specs.append(pl.BlockSpec((tm, tn), lambda i, j: (i, j)))
        args.append(hist_rows)
    return pl.pallas_call(
        functools.partial(_ffn_up_kernel, group=group, tail_rows=tail_rows),
        grid=(m // tm, nj),
        in_specs=in_specs,
        out_specs=[pl.BlockSpec((tm, tn), lambda i, j: (i, j)),
                   pl.BlockSpec((1, tail_rows, tn), lambda i, j: (i, 0, j))],
        out_shape=[jax.ShapeDtypeStruct((m, f), BF16),
                   jax.ShapeDtypeStruct((m // tm, tail_rows, f), F32)],
        scratch_shapes=scratch,
        compiler_params=_params("arbitrary", "arbitrary"),
    )(*args)


def _layer(x, wts, *, conv_hist=None, ffn_hist=None, paged=None):
    (norm1_g, w_in, dw_a_w, dw_a_b, ln_a_g, ln_a_b, w_proj_a, q_norm_g, k_norm_g, sb_bias,
     w_proj_b, w_out, norm2_g, w_up, dw_f_w, dw_f_b, w_down) = wts
    nb, t, d_model = x.shape
    m = nb * t
    d_conv = dw_a_w.shape[1]
    d_attn = w_proj_b.shape[0]
    d_ff = dw_f_w.shape[1]
    x2d = x.reshape(m, d_model)

    h = _rmsnorm(x2d, norm1_g)
    (a_glu,) = _in_proj(_glu_kernel, h, w_in, 0, d_conv, [F32], n_w=2, w_stride=d_conv)
    col = 2 * d_conv
    sb_bias2 = sb_bias * LOG2E
    (q16,) = _in_proj(functools.partial(_headnorm_kernel, scale=HEAD_DIM ** -0.5 * LOG2E, emit_f32=False),
                      h, w_in, col, d_attn, [BF16], extra=(q_norm_g.reshape(1, HEAD_DIM),))
    k32, k16 = _in_proj(functools.partial(_headnorm_kernel, scale=1.0, emit_f32=True),
                        h, w_in, col + d_attn, d_attn, [F32, BF16], extra=(k_norm_g.reshape(1, HEAD_DIM),))
    v32, v16 = _in_proj(_plain_kernel, h, w_in, col + 2 * d_attn, d_attn, [F32, BF16])
    (gates,) = _in_proj(_gate_kernel, h, w_in, col + 3 * d_attn, 2 * d_model, [BF16])

    if paged is None:
        act_a = _conv_branch_prompt(a_glu, dw_a_w, dw_a_b, ln_a_g, ln_a_b)
        conv_state = a_glu[m - (dw_a_w.shape[0] - 1):].reshape(1, -1, d_conv)
        o = _attn_prompt(q16, k16, v16, sb_bias2)
    else:
        a3 = a_glu.reshape(nb, t, d_conv)
        act_a = _conv_branch_sample(conv_hist, a3, dw_a_w, dw_a_b, ln_a_g, ln_a_b).reshape(m, d_conv)
        conv_state = jnp.concatenate([conv_hist, a3], axis=1)[:, t:]
        cache_k, cache_v, page_table = paged
        n_phys, page, nh = cache_k.shape[:3]
        o = _attn_sample(q16.reshape(nb, t, d_attn), k32.reshape(nb, t, d_attn), v32.reshape(nb, t, d_attn),
                         cache_k.reshape(n_phys, page * nh, HEAD_DIM), cache_v.reshape(n_phys, page * nh, HEAD_DIM),
                         page_table, sb_bias2).reshape(m, d_attn)

    merged = _merge(act_a, o, w_proj_a, w_proj_b, gates)
    x_mid = _residual_matmul(merged, w_out, x2d)

    h2 = _rmsnorm(x_mid, norm2_g)
    if paged is None:
        g, tails = _ffn_up(h2, w_up, dw_f_w, dw_f_b)
        ffn_state = tails[-1, SUBLANES - (dw_f_w.shape[0] - 1):].reshape(1, -1, d_ff)
    else:
        wf = dw_f_w.shape[0]
        hist_rows = jnp.pad(ffn_hist, ((0, 0), (0, t - (wf - 1)), (0, 0))).reshape(m, d_ff)
        g, tails = _ffn_up(h2, w_up, dw_f_w, dw_f_b, hist_rows=hist_rows, group=t)
        ffn_state = jnp.concatenate([ffn_hist, tails.reshape(nb, t, d_ff)], axis=1)[:, t:]
    y = _residual_matmul(g, w_down, x_mid)

    heads = (nb, t, d_attn // HEAD_DIM, HEAD_DIM)
    return (y.reshape(nb, t, d_model), k32.reshape(heads), v32.reshape(heads), conv_state, ffn_state)


def kernel(x_prompt, x_sample, cache_k, cache_v, state_conv, state_ffn_conv, page_table, norm1_g, w_in, dw_a_w, dw_a_b, ln_a_g, ln_a_b, w_proj_a, q_norm_g, k_norm_g, sb_bias, w_proj_b, w_out, norm2_g, w_up, dw_f_w, dw_f_b, w_down):
    assert x_prompt.shape[0] == 1
    wts = (norm1_g, w_in, dw_a_w, dw_a_b, ln_a_g, ln_a_b, w_proj_a, q_norm_g, k_norm_g, sb_bias,
           w_proj_b, w_out, norm2_g, w_up, dw_f_w, dw_f_b, w_down.astype(BF16))
    yp, k_p, v_p, conv_p, ffn_p = _layer(x_prompt, wts)
    ys, k_s, v_s, conv_s, ffn_s = _layer(x_sample, wts, conv_hist=state_conv, ffn_hist=state_ffn_conv,
                                         paged=(cache_k, cache_v, page_table))
    return (yp, ys, k_p, v_p, conv_p, ffn_p, k_s, v_s, conv_s, ffn_s)
```

```python
import functools
import math

import jax
import jax.numpy as jnp
from jax import lax
from jax.experimental import pallas as pl
from jax.experimental.pallas import tpu as pltpu

HEAD_DIM = 128
RMS_EPS = 1e-6
LN_EPS = 1e-5
LOG2E = math.log2(math.e)
MASKED_LOG2 = -1e30
CONV_HALO = 32
CONV_LANES = 128
SUBLANES = 8
HEAD_ROW_PAD = 4
V7X_VMEM_LIMIT = 56 * 1024 * 1024

F32 = jnp.float32
BF16 = jnp.bfloat16


def _largest_divisor(n, candidates):
    for c in candidates:
        if n % c == 0:
            return c
    raise ValueError(f"no tile in {candidates} divides {n}")


def _dot(a, b):
    return jnp.dot(a, b, preferred_element_type=F32)


def _matmul(a, w_ref):
    return jnp.dot(a, w_ref[...].astype(BF16), preferred_element_type=F32)


def _sigmoid(x):
    return 1.0 / (1.0 + jnp.exp(-x))


def _params(*sem):
    return pltpu.CompilerParams(dimension_semantics=sem, vmem_limit_bytes=V7X_VMEM_LIMIT)


def _rmsnorm_kernel(x_ref, g_ref, o_ref):
    x = x_ref[...]
    ms = jnp.mean(x * x, axis=-1, keepdims=True)
    o_ref[...] = (x * lax.rsqrt(ms + RMS_EPS) * g_ref[...]).astype(o_ref.dtype)


def _rmsnorm(x, g):
    m, d = x.shape
    tr = _largest_divisor(m, (256, 128, 64, 32, 16, 8))
    return pl.pallas_call(
        _rmsnorm_kernel,
        grid=(m // tr,),
        in_specs=[pl.BlockSpec((tr, d), lambda i: (i, 0)),
                  pl.BlockSpec((1, d), lambda i: (0, 0))],
        out_specs=pl.BlockSpec((tr, d), lambda i: (i, 0)),
        out_shape=jax.ShapeDtypeStruct((m, d), BF16),
        compiler_params=_params("parallel"),
    )(x, g.reshape(1, d))


def _glu_kernel(h_ref, wa_ref, wb_ref, o_ref):
    h = h_ref[...]
    o_ref[...] = _matmul(h, wa_ref) * _sigmoid(_matmul(h, wb_ref))


def _headnorm_kernel(h_ref, w_ref, g_ref, *o_refs, scale, emit_f32):
    acc = _matmul(h_ref[...], w_ref)
    g = g_ref[...]
    for hh in range(acc.shape[1] // HEAD_DIM):
        blk = acc[:, hh * HEAD_DIM:(hh + 1) * HEAD_DIM]
        ms = jnp.mean(blk * blk, axis=-1, keepdims=True)
        y = blk * lax.rsqrt(ms + RMS_EPS) * g
        cols = slice(hh * HEAD_DIM, (hh + 1) * HEAD_DIM)
        if emit_f32:
            o_refs[0][:, cols] = y
        o_refs[-1][:, cols] = (y * scale).astype(BF16)


def _plain_kernel(h_ref, w_ref, o32_ref, o16_ref):
    acc = _matmul(h_ref[...], w_ref)
    o32_ref[...] = acc
    o16_ref[...] = acc.astype(BF16)


def _gate_kernel(h_ref, w_ref, o_ref):
    o_ref[...] = _sigmoid(_matmul(h_ref[...], w_ref)).astype(BF16)


def _in_proj(kernel, h, w, col0, n_cols, out_dtypes, extra=(), n_w=1, w_stride=0):
    m, k = h.shape
    tm = _largest_divisor(m, (1024, 512, 256))
    tn = _largest_divisor(math.gcd(n_cols, col0, w_stride), (512, 256, 128) if n_w == 1 else (256, 128))
    w_specs = [pl.BlockSpec((k, tn), functools.partial(
        lambda i, j, off: (0, j + off), off=(col0 + s * w_stride) // tn)) for s in range(n_w)]
    extra_specs = [pl.BlockSpec(e.shape, lambda i, j: (0, 0)) for e in extra]
    outs = pl.pallas_call(
        kernel,
        grid=(m // tm, n_cols // tn),
        in_specs=[pl.BlockSpec((tm, k), lambda i, j: (i, 0))] + w_specs + extra_specs,
        out_specs=[pl.BlockSpec((tm, tn), lambda i, j: (i, j)) for _ in out_dtypes],
        out_shape=[jax.ShapeDtypeStruct((m, n_cols), dt) for dt in out_dtypes],
        compiler_params=_params("parallel", "parallel"),
    )(h, *([w] * n_w), *extra)
    return outs


def _conv_ln_silu(xp_ref, shift_ref, conv_ref, w_ref, b_ref, g_ref, beta_ref, o_ref, *, rows, base):
    width, c = w_ref.shape
    lane_chunk = shift_ref.shape[2]
    row_chunk = min(rows, 32)
    conv_rows = min(rows, 64)

    def chunk(ci, carry):
        lanes = pl.ds(pl.multiple_of(ci * lane_chunk, lane_chunk), lane_chunk)
        for b in range(min(SUBLANES, width)):
            span = rows + (len(range(b, width, SUBLANES)) - 1) * SUBLANES
            shift_ref[b, pl.ds(0, span), :] = xp_ref[pl.ds(base + b, span), lanes]
        wv = w_ref[:, lanes]
        bias = b_ref[:, lanes]
        for r0 in range(0, rows, conv_rows):
            acc = jnp.zeros((conv_rows, lane_chunk), F32) + bias
            for tap in range(width):
                b = tap % SUBLANES
                acc = acc + shift_ref[b, pl.ds(r0 + tap - b, conv_rows), :] * wv[tap:tap + 1]
            conv_ref[pl.ds(r0, conv_rows), lanes] = acc
        return carry

    lax.fori_loop(0, c // lane_chunk, chunk, 0)

    g = g_ref[...]
    beta = beta_ref[...]
    for r0 in range(0, rows, row_chunk):
        x = conv_ref[pl.ds(r0, row_chunk), :]
        mu = jnp.mean(x, axis=-1, keepdims=True)
        xc = x - mu
        var = jnp.mean(xc * xc, axis=-1, keepdims=True)
        y = xc * lax.rsqrt(var + LN_EPS) * g + beta
        o_ref[pl.ds(r0, row_chunk), :] = (y * _sigmoid(y)).astype(o_ref.dtype)


def _conv_prompt_kernel(halo_ref, x_ref, w_ref, b_ref, g_ref, beta_ref, o_ref, xp_ref, shift_ref, conv_ref, *, width):
    rows = x_ref.shape[0]
    first = pl.program_id(0) == 0
    xp_ref[pl.ds(0, CONV_HALO), :] = jnp.where(first, 0.0, halo_ref[...])
    xp_ref[pl.ds(CONV_HALO, rows), :] = x_ref[...]
    _conv_ln_silu(xp_ref, shift_ref, conv_ref, w_ref, b_ref, g_ref, beta_ref, o_ref,
                  rows=rows, base=CONV_HALO - (width - 1))


def _conv_sample_kernel(hist_ref, x_ref, w_ref, b_ref, g_ref, beta_ref, o_ref, xp_ref, shift_ref, conv_ref, *, width):
    rows = x_ref.shape[1]
    xp_ref[...] = jnp.zeros_like(xp_ref)
    xp_ref[pl.ds(0, width - 1), :] = hist_ref[0]
    xp_ref[pl.ds(width - 1, rows), :] = x_ref[0]
    _conv_ln_silu(xp_ref, shift_ref, conv_ref, w_ref, b_ref, g_ref, beta_ref, o_ref.at[0],
                  rows=rows, base=0)


def _conv_branch_prompt(a_glu, w, b, g, beta):
    t, c = a_glu.shape
    width = w.shape[0]
    assert width - 1 <= CONV_HALO
    tr = _largest_divisor(t, (256, 128, 64, 32))
    halo_blocks = tr // CONV_HALO
    vec = lambda v: v.reshape(1, c)
    return pl.pallas_call(
        functools.partial(_conv_prompt_kernel, width=width),
        grid=(t // tr,),
        in_specs=[pl.BlockSpec((CONV_HALO, c), lambda i: (jnp.maximum(i * halo_blocks - 1, 0), 0)),
                  pl.BlockSpec((tr, c), lambda i: (i, 0)),
                  pl.BlockSpec((width, c), lambda i: (0, 0))] +
                 [pl.BlockSpec((1, c), lambda i: (0, 0))] * 3,
        out_specs=pl.BlockSpec((tr, c), lambda i: (i, 0)),
        out_shape=jax.ShapeDtypeStruct((t, c), BF16),
        scratch_shapes=[pltpu.VMEM((CONV_HALO + tr, c), F32),
                        pltpu.VMEM((SUBLANES, CONV_HALO + tr, CONV_LANES), F32), pltpu.VMEM((tr, c), F32)],
        compiler_params=_params("parallel"),
    )(a_glu, a_glu, w, vec(b), vec(g), vec(beta))


def _conv_branch_sample(hist, a_glu, w, b, g, beta):
    nb, t, c = a_glu.shape
    width = w.shape[0]
    xp_rows = -(-(width - 1 + t + SUBLANES) // SUBLANES) * SUBLANES
    vec = lambda v: v.reshape(1, c)
    return pl.pallas_call(
        functools.partial(_conv_sample_kernel, width=width),
        grid=(nb,),
        in_specs=[pl.BlockSpec((1, width - 1, c), lambda i: (i, 0, 0)),
                  pl.BlockSpec((1, t, c), lambda i: (i, 0, 0)),
                  pl.BlockSpec((width, c), lambda i: (0, 0))] +
                 [pl.BlockSpec((1, c), lambda i: (0, 0))] * 3,
        out_specs=pl.BlockSpec((1, t, c), lambda i: (i, 0, 0)),
        out_shape=jax.ShapeDtypeStruct((nb, t, c), BF16),
        scratch_shapes=[pltpu.VMEM((xp_rows, c), F32),
                        pltpu.VMEM((SUBLANES, xp_rows, CONV_LANES), F32), pltpu.VMEM((t, c), F32)],
        compiler_params=_params("parallel"),
    )(hist, a_glu, w, vec(b), vec(g), vec(beta))


def _log2_gates(z2):
    sign_bit = jnp.uint32(1 << 31)
    neg_abs = lax.bitcast_convert_type(lax.bitcast_convert_type(z2, jnp.uint32) | sign_bit, F32)
    lp = jnp.log(1.0 + jnp.exp2(neg_abs)) * LOG2E
    log_beta = jnp.minimum(z2, 0.0) - lp
    return log_beta, log_beta - z2


def _attn_prompt_kernel(bias_ref, q_ref, k_ref, v_ref, u_ref, o_ref, *, tk, n_heads):
    grp = pl.program_id(0)
    qi = pl.program_id(1)
    tq = q_ref.shape[0]
    u = u_ref[...]
    sub = u.shape[0]
    n_sub = tk // sub

    heads = range(n_heads)

    def block_rows(kb):
        return pl.ds(pl.multiple_of(kb * tk, tk), tk)

    def gates(kb, mask):
        out = []
        for hh in heads:
            cols = slice(hh * HEAD_DIM, (hh + 1) * HEAD_DIM)
            z2 = lax.dot_general(q_ref[:, cols], k_ref[block_rows(kb), cols], (((1,), (1,)), ((), ())),
                                 preferred_element_type=F32) + bias_ref[grp * n_heads + hh]
            log_beta, log_1mb = _log2_gates(z2)
            if mask is not None:
                log_beta = jnp.where(mask, log_beta, MASKED_LOG2)
                log_1mb = jnp.where(mask, log_1mb, 0.0)
            out.append((log_beta, log_1mb.astype(BF16)))
        return tuple(out)

    def weigh(kb, pending, acc, run):
        tails = [[_dot(pending[hh][1][:, sb * sub:(sb + 1) * sub], u) for sb in range(n_sub)] for hh in heads]
        new_acc, new_run = [], []
        for hh in heads:
            log_beta, log_1mb = pending[hh]
            r = run[hh]
            weights = [None] * n_sub
            for sb in reversed(range(n_sub)):
                t = tails[hh][sb]
                w = log_beta[:, sb * sub:(sb + 1) * sub] + t + jnp.concatenate([r] * (sub // HEAD_DIM), axis=1)
                weights[sb] = jnp.exp2(w).astype(BF16)
                total = t[:, :1] + log_1mb[:, sb * sub:sb * sub + 1].astype(F32)
                r = r + jnp.broadcast_to(total, (tq, HEAD_DIM))
            v = v_ref[block_rows(kb), hh * HEAD_DIM:(hh + 1) * HEAD_DIM]
            new_acc.append(acc[hh] + _dot(jnp.concatenate(weights, axis=1), v))
            new_run.append(r)
        return tuple(new_acc), tuple(new_run)

    def visit(kb, state, mask):
        return weigh(kb, gates(kb, mask), *state)

    first_row = qi * tq
    diag = first_row // tk
    causal = (lax.broadcasted_iota(jnp.int32, (tq, tk), 1)
              < lax.broadcasted_iota(jnp.int32, (tq, tk), 0) + (first_row - diag * tk))
    zeros = tuple(jnp.zeros((tq, HEAD_DIM), F32) for _ in heads)
    state = visit(diag, (zeros, zeros), causal)
    acc, _ = lax.fori_loop(0, diag, lambda t, st: visit(diag - 1 - t, st, None), state)
    for hh in range(n_heads):
        o_ref[:, hh * HEAD_DIM:(hh + 1) * HEAD_DIM] = acc[hh].astype(o_ref.dtype)


def _attn_prompt(q, k, v, sb_bias2):
    t, d = q.shape
    nh = d // HEAD_DIM
    tq = tk = _largest_divisor(t, (512, 256, 128))
    sub = min(tk, 256)
    hb = _largest_divisor(nh, (4, 2, 1))
    sum_mat = (jnp.arange(sub)[:, None] > jnp.arange(sub)[None, :]).astype(BF16)
    return pl.pallas_call(
        functools.partial(_attn_prompt_kernel, tk=tk, n_heads=hb),
        grid=(nh // hb, t // tq),
        in_specs=[pl.BlockSpec(memory_space=pltpu.SMEM),
                  pl.BlockSpec((tq, hb * HEAD_DIM), lambda h, i: (i, h)),
                  pl.BlockSpec((t, hb * HEAD_DIM), lambda h, i: (0, h)),
                  pl.BlockSpec((t, hb * HEAD_DIM), lambda h, i: (0, h)),
                  pl.BlockSpec(sum_mat.shape, lambda h, i: (0, 0))],
        out_specs=pl.BlockSpec((tq, hb * HEAD_DIM), lambda h, i: (i, h)),
        out_shape=jax.ShapeDtypeStruct((t, d), BF16),
        compiler_params=_params("parallel", "parallel"),
    )(sb_bias2, q, k, v, sum_mat)


def _attn_sample_kernel(pt_ref, qt_ref, bias_ref, knew_ref, vnew_ref, ut_ref, *rest, n_slots, t_new):
    k_refs = rest[:n_slots]
    v_refs = rest[n_slots:2 * n_slots]
    o_ref = rest[2 * n_slots]
    qbd_ref, run_ref, acc_ref, pad_k_ref, pad_v_ref = rest[2 * n_slots + 1:2 * n_slots + 6]
    by_head_refs = rest[2 * n_slots + 6:]
    p = pl.program_id(1)
    page, d = pad_k_ref.shape
    ncol = qbd_ref.shape[1]
    nh = d // HEAD_DIM
    bias = bias_ref[...]
    ut = ut_ref[...]

    head_group = min(nh, SUBLANES)
    pitch = by_head_refs[0].shape[0] // nh

    def page_rows(ref, by_head_ref):
        for key in range(page):
            for h0 in range(0, nh, head_group):
                by_head_ref[pl.ds(h0 * pitch + key, head_group, stride=pitch), :] = (
                    ref[0, pl.ds(key * nh + h0, head_group), :])
        return jnp.concatenate([by_head_ref[pl.ds(hh * pitch, page), :] for hh in range(nh)],
                               axis=1).astype(BF16)

    def visit(ks, vs, mask):
        gates = [_log2_gates(_dot(k, qbd_ref[...]) + bias) for k in ks]
        log_1mb = [g[1] if mask is None else jnp.where(mask, g[1], 0.0) for g in gates]
        tails = [_dot(ut, l.astype(BF16)) for l in log_1mb]
        run = run_ref[...]
        weights = []
        for g, l, tail in zip(gates, log_1mb, tails):
            a = jnp.exp2(g[0] + tail + run)
            weights.append((a if mask is None else jnp.where(mask, a, 0.0)).astype(BF16))
            run = run + jnp.sum(l, axis=0, keepdims=True)
        acc_ref[...] += lax.dot_general(jnp.concatenate(weights, axis=0), jnp.concatenate(vs, axis=0),
                                        (((0,), (0,)), ((), ())), preferred_element_type=F32)
        run_ref[...] = run

    @pl.when(p == 0)
    def _():
        row_head = lax.broadcasted_iota(jnp.int32, (d, ncol), 0) // HEAD_DIM
        col_head = lax.broadcasted_iota(jnp.int32, (d, ncol), 1) // t_new
        qbd_ref[...] = jnp.where(row_head == col_head, qt_ref[0], jnp.zeros((), BF16))
        run_ref[...] = jnp.zeros_like(run_ref)
        acc_ref[...] = jnp.zeros_like(acc_ref)
        pad_k_ref[...] = jnp.zeros_like(pad_k_ref)
        pad_v_ref[...] = jnp.zeros_like(pad_v_ref)
        pad_k_ref[pl.ds(0, t_new), :] = knew_ref[0]
        pad_v_ref[pl.ds(0, t_new), :] = vnew_ref[0]
        key_t = lax.broadcasted_iota(jnp.int32, (page, ncol), 0)
        qry_t = lax.broadcasted_iota(jnp.int32, (page, ncol), 1) % t_new
        visit([pad_k_ref[...].astype(BF16)], [pad_v_ref[...].astype(BF16)], key_t < qry_t)

    visit([page_rows(k_refs[s], by_head_refs[2 * s]) for s in range(n_slots)],
          [page_rows(v_refs[s], by_head_refs[2 * s + 1]) for s in range(n_slots)], None)

    @pl.when(p == pl.num_programs(1) - 1)
    def _():
        for hh in range(nh):
            o_ref[0, :, hh * HEAD_DIM:(hh + 1) * HEAD_DIM] = acc_ref[
                hh * t_new:(hh + 1) * t_new, hh * HEAD_DIM:(hh + 1) * HEAD_DIM].astype(o_ref.dtype)


def _attn_sample(q, k_new, v_new, cache_k, cache_v, page_table, sb_bias2):
    nb, t_new, d = q.shape
    nh = d // HEAD_DIM
    n_pages = page_table.shape[1]
    page = cache_k.shape[1] // nh
    ncol = nh * t_new
    n_slots = _largest_divisor(n_pages, (4, 2, 1))
    q_t = jnp.tile(jnp.swapaxes(q, 1, 2), (1, 1, nh))
    bias_cols = jnp.repeat(sb_bias2, t_new).reshape(1, ncol)
    tri_t = (jnp.arange(page)[None, :] > jnp.arange(page)[:, None]).astype(BF16)

    def page_spec(s):
        return pl.BlockSpec((1, page * nh, HEAD_DIM),
                            lambda b, p, pt: (pt[b, n_pages - 1 - p * n_slots - s], 0, 0))

    grid_spec = pltpu.PrefetchScalarGridSpec(
        num_scalar_prefetch=1,
        grid=(nb, n_pages // n_slots),
        in_specs=[pl.BlockSpec((1, d, ncol), lambda b, p, pt: (b, 0, 0)),
                  pl.BlockSpec((1, ncol), lambda b, p, pt: (0, 0)),
                  pl.BlockSpec((1, t_new, d), lambda b, p, pt: (b, 0, 0)),
                  pl.BlockSpec((1, t_new, d), lambda b, p, pt: (b, 0, 0)),
                  pl.BlockSpec((page, page), lambda b, p, pt: (0, 0))] +
                 [page_spec(s) for s in range(n_slots)] * 2,
        out_specs=pl.BlockSpec((1, t_new, d), lambda b, p, pt: (b, 0, 0)),
        scratch_shapes=[pltpu.VMEM((d, ncol), BF16), pltpu.VMEM((1, ncol), F32),
                        pltpu.VMEM((ncol, d), F32), pltpu.VMEM((page, d), F32),
                        pltpu.VMEM((page, d), F32)] +
                       [pltpu.VMEM((nh * (page + HEAD_ROW_PAD), HEAD_DIM), F32)] * (2 * n_slots),
    )
    return pl.pallas_call(
        functools.partial(_attn_sample_kernel, n_slots=n_slots, t_new=t_new),
        grid_spec=grid_spec,
        out_shape=jax.ShapeDtypeStruct((nb, t_new, d), BF16),
        compiler_params=_params("parallel", "arbitrary"),
    )(page_table, q_t, bias_cols, k_new, v_new, tri_t,
      *([cache_k] * n_slots), *([cache_v] * n_slots))


def _merge_kernel(a_ref, o_ref, wa_ref, wb_ref, ga_ref, gb_ref, out_ref):
    ya = _matmul(a_ref[...], wa_ref)
    yb = _matmul(o_ref[...], wb_ref)
    out_ref[...] = (ga_ref[...].astype(F32) * ya + gb_ref[...].astype(F32) * yb).astype(out_ref.dtype)


def _merge(act_a, o, w_a, w_b, gates):
    m, ka = act_a.shape
    kb = o.shape[1]
    n = w_a.shape[1]
    tm = _largest_divisor(m, (1024, 512, 256))
    tn = _largest_divisor(n, (512, 256))
    return pl.pallas_call(
        _merge_kernel,
        grid=(m // tm, n // tn),
        in_specs=[pl.BlockSpec((tm, ka), lambda i, j: (i, 0)),
                  pl.BlockSpec((tm, kb), lambda i, j: (i, 0)),
                  pl.BlockSpec((ka, tn), lambda i, j: (0, j)),
                  pl.BlockSpec((kb, tn), lambda i, j: (0, j)),
                  pl.BlockSpec((tm, tn), lambda i, j: (i, j)),
                  pl.BlockSpec((tm, tn), lambda i, j: (i, j + n // tn))],
        out_specs=pl.BlockSpec((tm, tn), lambda i, j: (i, j)),
        out_shape=jax.ShapeDtypeStruct((m, n), BF16),
        compiler_params=_params("parallel", "parallel"),
    )(act_a, o, w_a, w_b, gates, gates)


def _residual_kernel(a_ref, w_ref, x_ref, o_ref):
    o_ref[...] = x_ref[...] + _matmul(a_ref[...], w_ref)


def _residual_matmul(a, w, x):
    m, k = a.shape
    n = w.shape[1]
    tm = _largest_divisor(m, (1024, 512, 256) if k <= 4096 else (512, 256))
    tn = _largest_divisor(n, (512, 256))
    return pl.pallas_call(
        _residual_kernel,
        grid=(m // tm, n // tn),
        in_specs=[pl.BlockSpec((tm, k), lambda i, j: (i, 0)),
                  pl.BlockSpec((k, tn), lambda i, j: (0, j)),
                  pl.BlockSpec((tm, tn), lambda i, j: (i, j))],
        out_specs=pl.BlockSpec((tm, tn), lambda i, j: (i, j)),
        out_shape=jax.ShapeDtypeStruct((m, n), F32),
        compiler_params=_params("parallel", "parallel"),
    )(a, w, x)


def _ffn_up_kernel(h_ref, wa_ref, wb_ref, cw_ref, cb_ref, *rest, group, tail_rows):
    if group is None:
        g_ref, tail_ref, carry_ref = rest
    else:
        hist_ref, g_ref, tail_ref = rest
    i = pl.program_id(0)
    j = pl.program_id(1)
    h = h_ref[...]
    f_a = _matmul(h, wa_ref)
    f_b = _matmul(h, wb_ref)
    tm = f_a.shape[0]
    width = cw_ref.shape[0]
    row = lax.broadcasted_iota(jnp.int32, f_a.shape, 0)
    conv = f_a * cw_ref[width - 1:width, :] + cb_ref[...]
    if group is None:
        @pl.when(i == 0)
        def _():
            carry_ref[j] = jnp.zeros(carry_ref.shape[1:], F32)

        prev_tail = carry_ref[j]
        for d in range(1, width):
            shifted = pltpu.roll(f_a, d, axis=0)
            for r in range(d):
                shifted = jnp.where(row == r, prev_tail[SUBLANES - d + r:SUBLANES - d + r + 1, :], shifted)
            conv = conv + shifted * cw_ref[width - 1 - d:width - d, :]
        carry_ref[j] = f_a[tm - SUBLANES:, :]
    else:
        hist = hist_ref[...]
        pos = row % group
        for d in range(1, width):
            shifted = jnp.where(pos < d, pltpu.roll(hist, (tm - (width - 1) + d) % tm, axis=0),
                                pltpu.roll(f_a, d, axis=0))
            conv = conv + shifted * cw_ref[width - 1 - d:width - d, :]
    g_ref[...] = (conv * _sigmoid(conv) * f_b).astype(g_ref.dtype)
    tail_ref[0] = f_a[tm - tail_rows:, :]


def _ffn_up(h2, w_up, conv_w, conv_b, hist_rows=None, group=None):
    m, k = h2.shape
    f = w_up.shape[1] // 2
    width = conv_w.shape[0]
    tm = _largest_divisor(m, (1024, 512, 256))
    tn = _largest_divisor(f, (256, 128))
    nj = f // tn
    tail_rows = SUBLANES if group is None else tm
    assert width - 1 <= SUBLANES
    in_specs = [pl.BlockSpec((tm, k), lambda i, j: (i, 0)),
                pl.BlockSpec((k, tn), lambda i, j: (0, j)),
                pl.BlockSpec((k, tn), lambda i, j: (0, j + nj)),
                pl.BlockSpec((width, tn), lambda i, j: (0, j)),
                pl.BlockSpec((1, tn), lambda i, j: (0, j))]
    args = [h2, w_up, w_up, conv_w, conv_b.reshape(1, f)]
    scratch = []
    if group is None:
        scratch = [pltpu.VMEM((nj, SUBLANES, tn), F32)]
    else:
        assert m == tm
        in_specs.append(pl.BlockSpec((tm, tn), lambda i, j: (i, j)))
        args.append(hist_rows)
    return pl.pallas_call(
        functools.partial(_ffn_up_kernel, group=group, tail_rows=tail_rows),
        grid=(m // tm, nj),
        in_specs=in_specs,
        out_specs=[pl.BlockSpec((tm, tn), lambda i, j: (i, j)),
                   pl.BlockSpec((1, tail_rows, tn), lambda i, j: (i, 0, j))],
        out_shape=[jax.ShapeDtypeStruct((m, f), BF16),
                   jax.ShapeDtypeStruct((m // tm, tail_rows, f), F32)],
        scratch_shapes=scratch,
        compiler_params=_params("arbitrary", "arbitrary"),
    )(*args)


def _layer(x, wts, *, conv_hist=None, ffn_hist=None, paged=None):
    (norm1_g, w_in, dw_a_w, dw_a_b, ln_a_g, ln_a_b, w_proj_a, q_norm_g, k_norm_g, sb_bias,
     w_proj_b, w_out, norm2_g, w_up, dw_f_w, dw_f_b, w_down) = wts
    nb, t, d_model = x.shape
    m = nb * t
    d_conv = dw_a_w.shape[1]
    d_attn = w_proj_b.shape[0]
    d_ff = dw_f_w.shape[1]
    x2d = x.reshape(m, d_model)

    h = _rmsnorm(x2d, norm1_g)
    (a_glu,) = _in_proj(_glu_kernel, h, w_in, 0, d_conv, [F32], n_w=2, w_stride=d_conv)
    col = 2 * d_conv
    sb_bias2 = sb_bias * LOG2E
    (q16,) = _in_proj(functools.partial(_headnorm_kernel, scale=HEAD_DIM ** -0.5 * LOG2E, emit_f32=False),
                      h, w_in, col, d_attn, [BF16], extra=(q_norm_g.reshape(1, HEAD_DIM),))
    k32, k16 = _in_proj(functools.partial(_headnorm_kernel, scale=1.0, emit_f32=True),
                        h, w_in, col + d_attn, d_attn, [F32, BF16], extra=(k_norm_g.reshape(1, HEAD_DIM),))
    v32, v16 = _in_proj(_plain_kernel, h, w_in, col + 2 * d_attn, d_attn, [F32, BF16])
    (gates,) = _in_proj(_gate_kernel, h, w_in, col + 3 * d_attn, 2 * d_model, [BF16])

    if paged is None:
        act_a = _conv_branch_prompt(a_glu, dw_a_w, dw_a_b, ln_a_g, ln_a_b)
        conv_state = a_glu[m - (dw_a_w.shape[0] - 1):].reshape(1, -1, d_conv)
        o = _attn_prompt(q16, k16, v16, sb_bias2)
    else:
        a3 = a_glu.reshape(nb, t, d_conv)
        act_a = _conv_branch_sample(conv_hist, a3, dw_a_w, dw_a_b, ln_a_g, ln_a_b).reshape(m, d_conv)
        conv_state = jnp.concatenate([conv_hist, a3], axis=1)[:, t:]
        cache_k, cache_v, page_table = paged
        n_phys, page, nh = cache_k.shape[:3]
        o = _attn_sample(q16.reshape(nb, t, d_attn), k32.reshape(nb, t, d_attn), v32.reshape(nb, t, d_attn),
                         cache_k.reshape(n_phys, page * nh, HEAD_DIM), cache_v.reshape(n_phys, page * nh, HEAD_DIM),
                         page_table, sb_bias2).reshape(m, d_attn)

    merged = _merge(act_a, o, w_proj_a, w_proj_b, gates)
    x_mid = _residual_matmul(merged, w_out, x2d)

    h2 = _rmsnorm(x_mid, norm2_g)
    if paged is None:
        g, tails = _ffn_up(h2, w_up, dw_f_w, dw_f_b)
        ffn_state = tails[-1, SUBLANES - (dw_f_w.shape[0] - 1):].reshape(1, -1, d_ff)
    else:
        wf = dw_f_w.shape[0]
        hist_rows = jnp.pad(ffn_hist, ((0, 0), (0, t - (wf - 1)), (0, 0))).reshape(m, d_ff)
        g, tails = _ffn_up(h2, w_up, dw_f_w, dw_f_b, hist_rows=hist_rows, group=t)
        ffn_state = jnp.concatenate([ffn_hist, tails.reshape(nb, t, d_ff)], axis=1)[:, t:]
    y = _residual_matmul(g, w_down, x_mid)

    heads = (nb, t, d_attn // HEAD_DIM, HEAD_DIM)
    return (y.reshape(nb, t, d_model), k32.reshape(heads), v32.reshape(heads), conv_state, ffn_state)


def kernel(x_prompt, x_sample, cache_k, cache_v, state_conv, state_ffn_conv, page_table, norm1_g, w_in, dw_a_w, dw_a_b, ln_a_g, ln_a_b, w_proj_a, q_norm_g, k_norm_g, sb_bias, w_proj_b, w_out, norm2_g, w_up, dw_f_w, dw_f_b, w_down):
    assert x_prompt.shape[0] == 1
    wts = (norm1_g, w_in, dw_a_w, dw_a_b, ln_a_g, ln_a_b, w_proj_a, q_norm_g, k_norm_g, sb_bias,
           w_proj_b, w_out, norm2_g, w_up, dw_f_w, dw_f_b, w_down.astype(BF16))
    yp, k_p, v_p, conv_p, ffn_p = _layer(x_prompt, wts)
    ys, k_s, v_s, conv_s, ffn_s = _layer(x_sample, wts, conv_hist=state_conv, ffn_hist=state_ffn_conv,
                                         paged=(cache_k, cache_v, page_table))
    return (yp, ys, k_p, v_p, conv_p, ffn_p, k_s, v_s, conv_s, ffn_s)
```

```python
import functools
import math

import jax
import jax.numpy as jnp
from jax import lax
from jax.experimental import pallas as pl
from jax.experimental.pallas import tpu as pltpu

HEAD_DIM = 128
RMS_EPS = 1e-6
LN_EPS = 1e-5
LOG2E = math.log2(math.e)
MASKED_LOG2 = -1e30
CONV_HALO = 32
CONV_LANES = 128
MAX_ROW_TILE = 1152
SUBLANES = 8
HEAD_ROW_PAD = 4
V7X_VMEM_LIMIT = 56 * 1024 * 1024

F32 = jnp.float32
BF16 = jnp.bfloat16


def _largest_divisor(n, candidates):
    for c in candidates:
        if n % c == 0:
            return c
    raise ValueError(f"no tile in {candidates} divides {n}")


def _dot(a, b):
    return jnp.dot(a, b, preferred_element_type=F32)


def _matmul(a, w_ref):
    return jnp.dot(a, w_ref[...].astype(BF16), preferred_element_type=F32)


def _sigmoid(x):
    return 1.0 / (1.0 + jnp.exp(-x))


def _params(*sem):
    return pltpu.CompilerParams(dimension_semantics=sem, vmem_limit_bytes=V7X_VMEM_LIMIT)


def _rmsnorm_kernel(x_ref, g_ref, o_ref):
    x = x_ref[...]
    ms = jnp.mean(x * x, axis=-1, keepdims=True)
    o_ref[...] = (x * lax.rsqrt(ms + RMS_EPS) * g_ref[...]).astype(o_ref.dtype)


def _rmsnorm(x, g):
    m, d = x.shape
    tr = _largest_divisor(m, (256, 128, 64, 32, 16, 8))
    return pl.pallas_call(
        _rmsnorm_kernel,
        grid=(m // tr,),
        in_specs=[pl.BlockSpec((tr, d), lambda i: (i, 0)),
                  pl.BlockSpec((1, d), lambda i: (0, 0))],
        out_specs=pl.BlockSpec((tr, d), lambda i: (i, 0)),
        out_shape=jax.ShapeDtypeStruct((m, d), BF16),
        compiler_params=_params("parallel"),
    )(x, g.reshape(1, d))


def _rmsnorm_pair_kernel(xa_ref, xb_ref, g_ref, o_ref, *, n_a):
    x = jnp.where(pl.program_id(0) < n_a, xa_ref[...], xb_ref[...])
    ms = jnp.mean(x * x, axis=-1, keepdims=True)
    o_ref[...] = (x * lax.rsqrt(ms + RMS_EPS) * g_ref[...]).astype(o_ref.dtype)


def _rmsnorm_pair(xa, xb, g):
    (ma, d), mb = xa.shape, xb.shape[0]
    tr = _largest_divisor(math.gcd(ma, mb), (256, 128, 64, 32, 16))
    n_a, n_b = ma // tr, mb // tr
    return pl.pallas_call(
        functools.partial(_rmsnorm_pair_kernel, n_a=n_a),
        grid=(n_a + n_b,),
        in_specs=[pl.BlockSpec((tr, d), lambda i: (jnp.minimum(i, n_a - 1), 0)),
                  pl.BlockSpec((tr, d), lambda i: (jnp.maximum(i - n_a, 0), 0)),
                  pl.BlockSpec((1, d), lambda i: (0, 0))],
        out_specs=pl.BlockSpec((tr, d), lambda i: (i, 0)),
        out_shape=jax.ShapeDtypeStruct((ma + mb, d), BF16),
        compiler_params=_params("parallel"),
    )(xa, xb, g.reshape(1, d))


def _glu_kernel(h_ref, wa_ref, wb_ref, o_ref):
    h = h_ref[...]
    o_ref[...] = _matmul(h, wa_ref) * _sigmoid(_matmul(h, wb_ref))


def _headnorm_kernel(h_ref, w_ref, g_ref, *o_refs, scale, emit_f32):
    acc = _matmul(h_ref[...], w_ref)
    g = g_ref[...]
    for hh in range(acc.shape[1] // HEAD_DIM):
        blk = acc[:, hh * HEAD_DIM:(hh + 1) * HEAD_DIM]
        ms = jnp.mean(blk * blk, axis=-1, keepdims=True)
        y = blk * lax.rsqrt(ms + RMS_EPS) * g
        cols = slice(hh * HEAD_DIM, (hh + 1) * HEAD_DIM)
        if emit_f32:
            o_refs[0][:, cols] = y
        o_refs[-1][:, cols] = (y * scale).astype(BF16)


def _plain_kernel(h_ref, w_ref, o32_ref, o16_ref):
    acc = _matmul(h_ref[...], w_ref)
    o32_ref[...] = acc
    o16_ref[...] = acc.astype(BF16)


def _gate_kernel(h_ref, w_ref, o_ref):
    o_ref[...] = _sigmoid(_matmul(h_ref[...], w_ref)).astype(BF16)


def _in_proj(kernel, h, w, col0, n_cols, out_dtypes, extra=(), n_w=1, w_stride=0):
    m, k = h.shape
    tm = max(t for t in range(32, MAX_ROW_TILE + 1, 32) if m % t == 0)
    tn = _largest_divisor(math.gcd(n_cols, col0, w_stride), (512, 256, 128) if n_w == 1 else (256, 128))
    w_specs = [pl.BlockSpec((k, tn), functools.partial(
        lambda i, j, off: (0, j + off), off=(col0 + s * w_stride) // tn)) for s in range(n_w)]
    extra_specs = [pl.BlockSpec(e.shape, lambda i, j: (0, 0)) for e in extra]
    outs = pl.pallas_call(
        kernel,
        grid=(m // tm, n_cols // tn),
        in_specs=[pl.BlockSpec((tm, k), lambda i, j: (i, 0))] + w_specs + extra_specs,
        out_specs=[pl.BlockSpec((tm, tn), lambda i, j: (i, j)) for _ in out_dtypes],
        out_shape=[jax.ShapeDtypeStruct((m, n_cols), dt) for dt in out_dtypes],
        compiler_params=_params("parallel", "parallel"),
    )(h, *([w] * n_w), *extra)
    return outs


def _conv_ln_silu(xp_ref, shift_ref, conv_ref, w_ref, b_ref, g_ref, beta_ref, o_ref, *, rows, base):
    width, c = w_ref.shape
    lane_chunk = shift_ref.shape[2]
    row_chunk = min(rows, 32)
    conv_rows = min(rows, 64)

    def chunk(ci, carry):
        lanes = pl.ds(pl.multiple_of(ci * lane_chunk, lane_chunk), lane_chunk)
        for b in range(min(SUBLANES, width)):
            span = rows + (len(range(b, width, SUBLANES)) - 1) * SUBLANES
            shift_ref[b, pl.ds(0, span), :] = xp_ref[pl.ds(base + b, span), lanes]
        wv = w_ref[:, lanes]
        bias = b_ref[:, lanes]
        for r0 in range(0, rows, conv_rows):
            acc = jnp.zeros((conv_rows, lane_chunk), F32) + bias
            for tap in range(width):
                b = tap % SUBLANES
                acc = acc + shift_ref[b, pl.ds(r0 + tap - b, conv_rows), :] * wv[tap:tap + 1]
            conv_ref[pl.ds(r0, conv_rows), lanes] = acc
        return carry

    lax.fori_loop(0, c // lane_chunk, chunk, 0)

    g = g_ref[...]
    beta = beta_ref[...]
    for r0 in range(0, rows, row_chunk):
        x = conv_ref[pl.ds(r0, row_chunk), :]
        mu = jnp.mean(x, axis=-1, keepdims=True)
        xc = x - mu
        var = jnp.mean(xc * xc, axis=-1, keepdims=True)
        y = xc * lax.rsqrt(var + LN_EPS) * g + beta
        o_ref[pl.ds(r0, row_chunk), :] = (y * _sigmoid(y)).astype(o_ref.dtype)


def _conv_prompt_kernel(halo_ref, x_ref, w_ref, b_ref, g_ref, beta_ref, o_ref, xp_ref, shift_ref, conv_ref, *, width):
    rows = x_ref.shape[0]
    first = pl.program_id(0) == 0
    xp_ref[pl.ds(0, CONV_HALO), :] = jnp.where(first, 0.0, halo_ref[...])
    xp_ref[pl.ds(CONV_HALO, rows), :] = x_ref[...]
    _conv_ln_silu(xp_ref, shift_ref, conv_ref, w_ref, b_ref, g_ref, beta_ref, o_ref,
                  rows=rows, base=CONV_HALO - (width - 1))


def _conv_sample_kernel(hist_ref, x_ref, w_ref, b_ref, g_ref, beta_ref, o_ref, xp_ref, shift_ref, conv_ref, *, width):
    rows = x_ref.shape[1]
    xp_ref[...] = jnp.zeros_like(xp_ref)
    xp_ref[pl.ds(0, width - 1), :] = hist_ref[0]
    xp_ref[pl.ds(width - 1, rows), :] = x_ref[0]
    _conv_ln_silu(xp_ref, shift_ref, conv_ref, w_ref, b_ref, g_ref, beta_ref, o_ref.at[0],
                  rows=rows, base=0)


def _conv_branch_prompt(a_glu, t, w, b, g, beta):
    c = a_glu.shape[1]
    width = w.shape[0]
    assert width - 1 <= CONV_HALO
    tr = _largest_divisor(t, (256, 128, 64, 32))
    halo_blocks = tr // CONV_HALO
    vec = lambda v: v.reshape(1, c)
    return pl.pallas_call(
        functools.partial(_conv_prompt_kernel, width=width),
        grid=(t // tr,),
        in_specs=[pl.BlockSpec((CONV_HALO, c), lambda i: (jnp.maximum(i * halo_blocks - 1, 0), 0)),
                  pl.BlockSpec((tr, c), lambda i: (i, 0)),
                  pl.BlockSpec((width, c), lambda i: (0, 0))] +
                 [pl.BlockSpec((1, c), lambda i: (0, 0))] * 3,
        out_specs=pl.BlockSpec((tr, c), lambda i: (i, 0)),
        out_shape=jax.ShapeDtypeStruct((t, c), BF16),
        scratch_shapes=[pltpu.VMEM((CONV_HALO + tr, c), F32),
                        pltpu.VMEM((SUBLANES, CONV_HALO + tr, CONV_LANES), F32), pltpu.VMEM((tr, c), F32)],
        compiler_params=_params("parallel"),
    )(a_glu, a_glu, w, vec(b), vec(g), vec(beta))


def _conv_branch_sample(hist, a_glu, w, b, g, beta):
    nb, t, c = a_glu.shape
    width = w.shape[0]
    xp_rows = -(-(width - 1 + t + SUBLANES) // SUBLANES) * SUBLANES
    vec = lambda v: v.reshape(1, c)
    return pl.pallas_call(
        functools.partial(_conv_sample_kernel, width=width),
        grid=(nb,),
        in_specs=[pl.BlockSpec((1, width - 1, c), lambda i: (i, 0, 0)),
                  pl.BlockSpec((1, t, c), lambda i: (i, 0, 0)),
                  pl.BlockSpec((width, c), lambda i: (0, 0))] +
                 [pl.BlockSpec((1, c), lambda i: (0, 0))] * 3,
        out_specs=pl.BlockSpec((1, t, c), lambda i: (i, 0, 0)),
        out_shape=jax.ShapeDtypeStruct((nb, t, c), BF16),
        scratch_shapes=[pltpu.VMEM((xp_rows, c), F32),
                        pltpu.VMEM((SUBLANES, xp_rows, CONV_LANES), F32), pltpu.VMEM((t, c), F32)],
        compiler_params=_params("parallel"),
    )(hist, a_glu, w, vec(b), vec(g), vec(beta))


def _log2_gates(z2):
    sign_bit = jnp.uint32(1 << 31)
    neg_abs = lax.bitcast_convert_type(lax.bitcast_convert_type(z2, jnp.uint32) | sign_bit, F32)
    lp = jnp.log(1.0 + jnp.exp2(neg_abs)) * LOG2E
    log_beta = jnp.minimum(z2, 0.0) - lp
    return log_beta, log_beta - z2


def _attn_prompt_kernel(bias_ref, q_ref, k_ref, v_ref, u_ref, o_ref, *, tk, n_heads):
    grp = pl.program_id(0)
    qi = pl.program_id(1)
    tq = q_ref.shape[0]
    u = u_ref[...]
    sub = u.shape[0]
    n_sub = tk // sub

    heads = range(n_heads)

    def block_rows(kb):
        return pl.ds(pl.multiple_of(kb * tk, tk), tk)

    def gates(kb, mask):
        out = []
        for hh in heads:
            cols = slice(hh * HEAD_DIM, (hh + 1) * HEAD_DIM)
            z2 = lax.dot_general(q_ref[:, cols], k_ref[block_rows(kb), cols], (((1,), (1,)), ((), ())),
                                 preferred_element_type=F32) + bias_ref[grp * n_heads + hh]
            log_beta, log_1mb = _log2_gates(z2)
            if mask is not None:
                log_beta = jnp.where(mask, log_beta, MASKED_LOG2)
                log_1mb = jnp.where(mask, log_1mb, 0.0)
            out.append((log_beta, log_1mb.astype(BF16)))
        return tuple(out)

    def weigh(kb, pending, acc, run):
        tails = [[_dot(pending[hh][1][:, sb * sub:(sb + 1) * sub], u) for sb in range(n_sub)] for hh in heads]
        new_acc, new_run = [], []
        for hh in heads:
            log_beta, log_1mb = pending[hh]
            r = run[hh]
            weights = [None] * n_sub
            for sb in reversed(range(n_sub)):
                t = tails[hh][sb]
                w = log_beta[:, sb * sub:(sb + 1) * sub] + t + jnp.concatenate([r] * (sub // HEAD_DIM), axis=1)
                weights[sb] = jnp.exp2(w).astype(BF16)
                total = t[:, :1] + log_1mb[:, sb * sub:sb * sub + 1].astype(F32)
                r = r + jnp.broadcast_to(total, (tq, HEAD_DIM))
            v = v_ref[block_rows(kb), hh * HEAD_DIM:(hh + 1) * HEAD_DIM]
            new_acc.append(acc[hh] + _dot(jnp.concatenate(weights, axis=1), v))
            new_run.append(r)
        return tuple(new_acc), tuple(new_run)

    def visit(kb, state, mask):
        return weigh(kb, gates(kb, mask), *state)

    first_row = qi * tq
    diag = first_row // tk
    causal = (lax.broadcasted_iota(jnp.int32, (tq, tk), 1)
              < lax.broadcasted_iota(jnp.int32, (tq, tk), 0) + (first_row - diag * tk))
    zeros = tuple(jnp.zeros((tq, HEAD_DIM), F32) for _ in heads)
    state = visit(diag, (zeros, zeros), causal)
    acc, _ = lax.fori_loop(0, diag, lambda t, st: visit(diag - 1 - t, st, None), state)
    for hh in range(n_heads):
        o_ref[:, hh * HEAD_DIM:(hh + 1) * HEAD_DIM] = acc[hh].astype(o_ref.dtype)


def _attn_prompt(q, k, v, t, sb_bias2):
    d = q.shape[1]
    nh = d // HEAD_DIM
    tq = tk = _largest_divisor(t, (512, 256, 128))
    sub = min(tk, 256)
    hb = _largest_divisor(nh, (4, 2, 1))
    sum_mat = (jnp.arange(sub)[:, None] > jnp.arange(sub)[None, :]).astype(BF16)
    return pl.pallas_call(
        functools.partial(_attn_prompt_kernel, tk=tk, n_heads=hb),
        grid=(nh // hb, t // tq),
        in_specs=[pl.BlockSpec(memory_space=pltpu.SMEM),
                  pl.BlockSpec((tq, hb * HEAD_DIM), lambda h, i: (i, h)),
                  pl.BlockSpec((t, hb * HEAD_DIM), lambda h, i: (0, h)),
                  pl.BlockSpec((t, hb * HEAD_DIM), lambda h, i: (0, h)),
                  pl.BlockSpec(sum_mat.shape, lambda h, i: (0, 0))],
        out_specs=pl.BlockSpec((tq, hb * HEAD_DIM), lambda h, i: (i, h)),
        out_shape=jax.ShapeDtypeStruct((t, d), BF16),
        compiler_params=_params("parallel", "parallel"),
    )(sb_bias2, q, k, v, sum_mat)


def _attn_sample_kernel(pt_ref, qt_ref, bias_ref, knew_ref, vnew_ref, ut_ref, *rest, n_slots, t_new):
    k_refs = rest[:n_slots]
    v_refs = rest[n_slots:2 * n_slots]
    o_ref = rest[2 * n_slots]
    qbd_ref, run_ref, acc_ref, pad_k_ref, pad_v_ref = rest[2 * n_slots + 1:2 * n_slots + 6]
    by_head_refs = rest[2 * n_slots + 6:]
    p = pl.program_id(1)
    page, d = pad_k_ref.shape
    ncol = qbd_ref.shape[1]
    nh = d // HEAD_DIM
    bias = bias_ref[...]
    ut = ut_ref[...]

    head_group = min(nh, SUBLANES)
    pitch = by_head_refs[0].shape[0] // nh

    def page_rows(ref, by_head_ref):
        for key in range(page):
            for h0 in range(0, nh, head_group):
                by_head_ref[pl.ds(h0 * pitch + key, head_group, stride=pitch), :] = (
                    ref[0, pl.ds(key * nh + h0, head_group), :])
        return jnp.concatenate([by_head_ref[pl.ds(hh * pitch, page), :] for hh in range(nh)],
                               axis=1).astype(BF16)

    def visit(ks, vs, mask):
        gates = [_log2_gates(_dot(k, qbd_ref[...]) + bias) for k in ks]
        log_1mb = [g[1] if mask is None else jnp.where(mask, g[1], 0.0) for g in gates]
        tails = [_dot(ut, l.astype(BF16)) for l in log_1mb]
        run = run_ref[...]
        weights = []
        for g, l, tail in zip(gates, log_1mb, tails):
            a = jnp.exp2(g[0] + tail + run)
            weights.append((a if mask is None else jnp.where(mask, a, 0.0)).astype(BF16))
            run = run + jnp.sum(l, axis=0, keepdims=True)
        acc_ref[...] += lax.dot_general(jnp.concatenate(weights, axis=0), jnp.concatenate(vs, axis=0),
                                        (((0,), (0,)), ((), ())), preferred_element_type=F32)
        run_ref[...] = run

    @pl.when(p == 0)
    def _():
        row_head = lax.broadcasted_iota(jnp.int32, (d, ncol), 0) // HEAD_DIM
        col_head = lax.broadcasted_iota(jnp.int32, (d, ncol), 1) // t_new
        qbd_ref[...] = jnp.where(row_head == col_head, qt_ref[0], jnp.zeros((), BF16))
        run_ref[...] = jnp.zeros_like(run_ref)
        acc_ref[...] = jnp.zeros_like(acc_ref)
        pad_k_ref[...] = jnp.zeros_like(pad_k_ref)
        pad_v_ref[...] = jnp.zeros_like(pad_v_ref)
        pad_k_ref[pl.ds(0, t_new), :] = knew_ref[0]
        pad_v_ref[pl.ds(0, t_new), :] = vnew_ref[0]
        key_t = lax.broadcasted_iota(jnp.int32, (page, ncol), 0)
        qry_t = lax.broadcasted_iota(jnp.int32, (page, ncol), 1) % t_new
        visit([pad_k_ref[...].astype(BF16)], [pad_v_ref[...].astype(BF16)], key_t < qry_t)

    visit([page_rows(k_refs[s], by_head_refs[2 * s]) for s in range(n_slots)],
          [page_rows(v_refs[s], by_head_refs[2 * s + 1]) for s in range(n_slots)], None)

    @pl.when(p == pl.num_programs(1) - 1)
    def _():
        for hh in range(nh):
            o_ref[0, :, hh * HEAD_DIM:(hh + 1) * HEAD_DIM] = acc_ref[
                hh * t_new:(hh + 1) * t_new, hh * HEAD_DIM:(hh + 1) * HEAD_DIM].astype(o_ref.dtype)


def _attn_sample(q, k_new, v_new, cache_k, cache_v, page_table, sb_bias2):
    nb, t_new, d = q.shape
    nh = d // HEAD_DIM
    n_pages = page_table.shape[1]
    page = cache_k.shape[1] // nh
    ncol = nh * t_new
    n_slots = _largest_divisor(n_pages, (4, 2, 1))
    q_t = jnp.tile(jnp.swapaxes(q, 1, 2), (1, 1, nh))
    bias_cols = jnp.repeat(sb_bias2, t_new).reshape(1, ncol)
    tri_t = (jnp.arange(page)[None, :] > jnp.arange(page)[:, None]).astype(BF16)

    def page_spec(s):
        return pl.BlockSpec((1, page * nh, HEAD_DIM),
                            lambda b, p, pt: (pt[b, n_pages - 1 - p * n_slots - s], 0, 0))

    grid_spec = pltpu.PrefetchScalarGridSpec(
        num_scalar_prefetch=1,
        grid=(nb, n_pages // n_slots),
        in_specs=[pl.BlockSpec((1, d, ncol), lambda b, p, pt: (b, 0, 0)),
                  pl.BlockSpec((1, ncol), lambda b, p, pt: (0, 0)),
                  pl.BlockSpec((1, t_new, d), lambda b, p, pt: (b, 0, 0)),
                  pl.BlockSpec((1, t_new, d), lambda b, p, pt: (b, 0, 0)),
                  pl.BlockSpec((page, page), lambda b, p, pt: (0, 0))] +
                 [page_spec(s) for s in range(n_slots)] * 2,
        out_specs=pl.BlockSpec((1, t_new, d), lambda b, p, pt: (b, 0, 0)),
        scratch_shapes=[pltpu.VMEM((d, ncol), BF16), pltpu.VMEM((1, ncol), F32),
                        pltpu.VMEM((ncol, d), F32), pltpu.VMEM((page, d), F32),
                        pltpu.VMEM((page, d), F32)] +
                       [pltpu.VMEM((nh * (page + HEAD_ROW_PAD), HEAD_DIM), F32)] * (2 * n_slots),
    )
    return pl.pallas_call(
        functools.partial(_attn_sample_kernel, n_slots=n_slots, t_new=t_new),
        grid_spec=grid_spec,
        out_shape=jax.ShapeDtypeStruct((nb, t_new, d), BF16),
        compiler_params=_params("parallel", "arbitrary"),
    )(page_table, q_t, bias_cols, k_new, v_new, tri_t,
      *([cache_k] * n_slots), *([cache_v] * n_slots))


def _merge_kernel(a_ref, o_ref, wa_ref, wb_ref, ga_ref, gb_ref, out_ref):
    ya = _matmul(a_ref[...], wa_ref)
    yb = _matmul(o_ref[...], wb_ref)
    out_ref[...] = (ga_ref[...].astype(F32) * ya + gb_ref[...].astype(F32) * yb).astype(out_ref.dtype)


def _merge(act_a, o, w_a, w_b, gates, row0):
    m, ka = act_a.shape
    kb = o.shape[1]
    n = w_a.shape[1]
    tm = _largest_divisor(math.gcd(m, row0), (1024, 512, 256))
    tn = _largest_divisor(n, (512, 256))
    i0 = row0 // tm
    return pl.pallas_call(
        _merge_kernel,
        grid=(m // tm, n // tn),
        in_specs=[pl.BlockSpec((tm, ka), lambda i, j: (i, 0)),
                  pl.BlockSpec((tm, kb), lambda i, j: (i, 0)),
                  pl.BlockSpec((ka, tn), lambda i, j: (0, j)),
                  pl.BlockSpec((kb, tn), lambda i, j: (0, j)),
                  pl.BlockSpec((tm, tn), lambda i, j: (i + i0, j)),
                  pl.BlockSpec((tm, tn), lambda i, j: (i + i0, j + n // tn))],
        out_specs=pl.BlockSpec((tm, tn), lambda i, j: (i, j)),
        out_shape=jax.ShapeDtypeStruct((m, n), BF16),
        compiler_params=_params("parallel", "parallel"),
    )(act_a, o, w_a, w_b, gates, gates)


def _residual_kernel(a_ref, w_ref, x_ref, o_ref):
    o_ref[...] = x_ref[...] + _matmul(a_ref[...], w_ref)


def _residual_matmul(a, w, x):
    m, k = a.shape
    n = w.shape[1]
    tm = _largest_divisor(m, (1024, 512, 256) if k <= 4096 else (512, 256))
    tn = _largest_divisor(n, (512, 256))
    return pl.pallas_call(
        _residual_kernel,
        grid=(m // tm, n // tn),
        in_specs=[pl.BlockSpec((tm, k), lambda i, j: (i, 0)),
                  pl.BlockSpec((k, tn), lambda i, j: (0, j)),
                  pl.BlockSpec((tm, tn), lambda i, j: (i, j))],
        out_specs=pl.BlockSpec((tm, tn), lambda i, j: (i, j)),
        out_shape=jax.ShapeDtypeStruct((m, n), F32),
        compiler_params=_params("parallel", "parallel"),
    )(a, w, x)


def _ffn_up_kernel(h_ref, wa_ref, wb_ref, cw_ref, cb_ref, *rest, group, tail_rows):
    if group is None:
        g_ref, tail_ref, carry_ref = rest
    else:
        hist_ref, g_ref, tail_ref = rest
    i = pl.program_id(0)
    j = pl.program_id(1)
    h = h_ref[...]
    f_a = _matmul(h, wa_ref)
    f_b = _matmul(h, wb_ref)
    tm = f_a.shape[0]
    width = cw_ref.shape[0]
    row = lax.broadcasted_iota(jnp.int32, f_a.shape, 0)
    conv = f_a * cw_ref[width - 1:width, :] + cb_ref[...]
    if group is None:
        @pl.when(i == 0)
        def _():
            carry_ref[j] = jnp.zeros(carry_ref.shape[1:], F32)

        prev_tail = carry_ref[j]
        for d in range(1, width):
            shifted = pltpu.roll(f_a, d, axis=0)
            for r in range(d):
                shifted = jnp.where(row == r, prev_tail[SUBLANES - d + r:SUBLANES - d + r + 1, :], shifted)
            conv = conv + shifted * cw_ref[width - 1 - d:width - d, :]
        carry_ref[j] = f_a[tm - SUBLANES:, :]
    else:
        hist = hist_ref[...]
        pos = row % group
        for d in range(1, width):
            shifted = jnp.where(pos < d, pltpu.roll(hist, (tm - (width - 1) + d) % tm, axis=0),
                                pltpu.roll(f_a, d, axis=0))
            conv = conv + shifted * cw_ref[width - 1 - d:width - d, :]
    g_ref[...] = (conv * _sigmoid(conv) * f_b).astype(g_ref.dtype)
    tail_ref[0] = f_a[tm - tail_rows:, :]


def _ffn_up(h2, w_up, conv_w, conv_b, hist_rows=None, group=None):
    m, k = h2.shape
    f = w_up.shape[1] // 2
    width = conv_w.shape[0]
    tm = _largest_divisor(m, (1024, 512, 256))
    tn = _largest_divisor(f, (256, 128))
    nj = f // tn
    tail_rows = SUBLANES if group is None else tm
    assert width - 1 <= SUBLANES
    in_specs = [pl.BlockSpec((tm, k), lambda i, j: (i, 0)),
                pl.BlockSpec((k, tn), lambda i, j: (0, j)),
                pl.BlockSpec((k, tn), lambda i, j: (0, j + nj)),
                pl.BlockSpec((width, tn), lambda i, j: (0, j)),
                pl.BlockSpec((1, tn), lambda i, j: (0, j))]
    args = [h2, w_up, w_up, conv_w, conv_b.reshape(1, f)]
    scratch = []
    if group is None:
        scratch = [pltpu.VMEM((nj, SUBLANES, tn), F32)]
    else:
        assert m == tm
        in_specs.append(pl.BlockSpec((tm, tn), lambda i, j: (i, j)))
        args.append(hist_rows)
    return pl.pallas_call(
        functools.partial(_ffn_up_kernel, group=group, tail_rows=tail_rows),
        grid=(m // tm, nj),
        in_specs=in_specs,
        out_specs=[pl.BlockSpec((tm, tn), lambda i, j: (i, j)),
                   pl.BlockSpec((1, tail_rows, tn), lambda i, j: (i, 0, j))],
        out_shape=[jax.ShapeDtypeStruct((m, f), BF16),
                   jax.ShapeDtypeStruct((m // tm, tail_rows, f), F32)],
        scratch_shapes=scratch,
        compiler_params=_params("arbitrary", "arbitrary"),
    )(*args)


def _channel_mixer(x_mid, wts, ffn_hist):
    norm2_g, w_up, dw_f_w, dw_f_b, w_down = wts
    nb, t, d_model = x_mid.shape
    m = nb * t
    d_ff = dw_f_w.shape[1]
    wf = dw_f_w.shape[0]
    x2d = x_mid.reshape(m, d_model)
    h2 = _rmsnorm(x2d, norm2_g)
    if ffn_hist is None:
        g, tails = _ffn_up(h2, w_up, dw_f_w, dw_f_b)
        ffn_state = tails[-1, SUBLANES - (wf - 1):].reshape(1, -1, d_ff)
    else:
        hist_rows = jnp.pad(ffn_hist, ((0, 0), (0, t - (wf - 1)), (0, 0))).reshape(m, d_ff)
        g, tails = _ffn_up(h2, w_up, dw_f_w, dw_f_b, hist_rows=hist_rows, group=t)
        ffn_state = jnp.concatenate([ffn_hist, tails.reshape(nb, t, d_ff)], axis=1)[:, t:]
    y = _residual_matmul(g, w_down, x2d)
    return y.reshape(nb, t, d_model), ffn_state


def kernel(x_prompt, x_sample, cache_k, cache_v, state_conv, state_ffn_conv, page_table, norm1_g, w_in, dw_a_w, dw_a_b, ln_a_g, ln_a_b, w_proj_a, q_norm_g, k_norm_g, sb_bias, w_proj_b, w_out, norm2_g, w_up, dw_f_w, dw_f_b, w_down):
    assert x_prompt.shape[0] == 1
    t, d_model = x_prompt.shape[1:]
    nb, ts, _ = x_sample.shape
    ms = nb * ts
    d_conv = dw_a_w.shape[1]
    d_attn = w_proj_b.shape[0]
    nh = d_attn // HEAD_DIM
    xp2d = x_prompt.reshape(t, d_model)
    xs2d = x_sample.reshape(ms, d_model)

    h = _rmsnorm_pair(xp2d, xs2d, norm1_g)
    (a_glu,) = _in_proj(_glu_kernel, h, w_in, 0, d_conv, [F32], n_w=2, w_stride=d_conv)
    col = 2 * d_conv
    sb_bias2 = sb_bias * LOG2E
    (q16,) = _in_proj(functools.partial(_headnorm_kernel, scale=HEAD_DIM ** -0.5 * LOG2E, emit_f32=False),
                      h, w_in, col, d_attn, [BF16], extra=(q_norm_g.reshape(1, HEAD_DIM),))
    k32, k16 = _in_proj(functools.partial(_headnorm_kernel, scale=1.0, emit_f32=True),
                        h, w_in, col + d_attn, d_attn, [F32, BF16], extra=(k_norm_g.reshape(1, HEAD_DIM),))
    v32, v16 = _in_proj(_plain_kernel, h, w_in, col + 2 * d_attn, d_attn, [F32, BF16])
    (gates,) = _in_proj(_gate_kernel, h, w_in, col + 3 * d_attn, 2 * d_model, [BF16])

    act_p = _conv_branch_prompt(a_glu, t, dw_a_w, dw_a_b, ln_a_g, ln_a_b)
    conv_p = a_glu[t - (dw_a_w.shape[0] - 1):t].reshape(1, -1, d_conv)
    o_p = _attn_prompt(q16, k16, v16, t, sb_bias2)
    merged_p = _merge(act_p, o_p, w_proj_a, w_proj_b, gates, 0)
    x_mid_p = _residual_matmul(merged_p, w_out, xp2d)

    a_s = a_glu[t:].reshape(nb, ts, d_conv)
    act_s = _conv_branch_sample(state_conv, a_s, dw_a_w, dw_a_b, ln_a_g, ln_a_b).reshape(ms, d_conv)
    conv_s = jnp.concatenate([state_conv, a_s], axis=1)[:, ts:]
    n_phys, page = cache_k.shape[:2]
    o_s = _attn_sample(q16[t:].reshape(nb, ts, d_attn), k32[t:].reshape(nb, ts, d_attn),
                       v32[t:].reshape(nb, ts, d_attn),
                       cache_k.reshape(n_phys, page * nh, HEAD_DIM), cache_v.reshape(n_phys, page * nh, HEAD_DIM),
                       page_table, sb_bias2).reshape(ms, d_attn)
    merged_s = _merge(act_s, o_s, w_proj_a, w_proj_b, gates, t)
    x_mid_s = _residual_matmul(merged_s, w_out, xs2d)

    mixer = (norm2_g, w_up, dw_f_w, dw_f_b, w_down.astype(BF16))
    y_p, ffn_p = _channel_mixer(x_mid_p.reshape(1, t, d_model), mixer, None)
    y_s, ffn_s = _channel_mixer(x_mid_s.reshape(nb, ts, d_model), mixer, state_ffn_conv)

    k_p, v_p = (a[:t].reshape(1, t, nh, HEAD_DIM) for a in (k32, v32))
    k_s, v_s = (a[t:].reshape(nb, ts, nh, HEAD_DIM) for a in (k32, v32))
    return (y_p, y_s, k_p, v_p, conv_p, ffn_p, k_s, v_s, conv_s, ffn_s)
```
